```python
import math
import jax, jax.numpy as jnp
from jax import lax
import numpy as np

D_MODEL = 1024
BATCH = 8
SEQ = 2048
DEPTH = 4
DEC_BATCH = 128
DEC_SEQ = 1
PAST_LEN = 2048
PAGE_SIZE = 128

MIX_WIDTH = D_MODEL // 2
N_BRANCH = 3
SSM_WIDTH = MIX_WIDTH
SSM_GROUP_CH = 16
SSM_GROUPS = SSM_WIDTH // SSM_GROUP_CH
SSM_STATE = 64
DN_DK = 128
DN_DV = 128
DN_HEADS = MIX_WIDTH // DN_DV
DN_WIDTH = DN_HEADS * DN_DV
DN_QKV = DN_HEADS * (2 * DN_DK + DN_DV)
CONV_W = 4
DN_CHUNK = 64
DA_DH = 64
DA_DV = 2 * DA_DH
DA_HEADS = MIX_WIDTH // DA_DV
DA_WIDTH = DA_HEADS * DA_DV
DA_QKV = DA_HEADS * (4 * DA_DH + DA_DV)
Q_BLOCK = 128
N_EXPERTS = 16
N_EXPERT_GROUPS = 4
EXPERTS_PER_GROUP = N_EXPERTS // N_EXPERT_GROUPS
TOP_K = 2
D_EXPERT = 512
N_IN = SSM_WIDTH + DN_QKV + 2 * DN_HEADS + DN_WIDTH + DA_QKV + N_BRANCH * D_MODEL
DEEPNORM_ALPHA = (2 * DEPTH) ** 0.25
DEEPNORM_BETA = (8 * DEPTH) ** -0.25
LN_EPS = 1e-5
RMS_EPS = 1e-6

kernel_name = 'hybrid_s5_gdn_diffattn_grouped_moe_step'


def _in_splits():
    sizes = [SSM_WIDTH, DN_QKV, DN_HEADS, DN_HEADS, DN_WIDTH, DA_QKV, N_BRANCH * D_MODEL]
    return [int(s) for s in np.cumsum(sizes)[:-1]]


def layer_norm(x, g, b):
    xf = x.astype(jnp.float32)
    mu = jnp.mean(xf, axis=-1, keepdims=True)
    var = jnp.mean(jnp.square(xf - mu), axis=-1, keepdims=True)
    return ((xf - mu) * lax.rsqrt(var + LN_EPS) * g.astype(jnp.float32) + b.astype(jnp.float32)).astype(x.dtype)


def rms_norm(x, g):
    return x * lax.rsqrt(jnp.mean(jnp.square(x), axis=-1, keepdims=True) + RMS_EPS) * g.astype(jnp.float32)


def l2_normalize(x):
    return x * lax.rsqrt(jnp.sum(jnp.square(x), axis=-1, keepdims=True) + RMS_EPS)


def ssm_branch(u, h0, a_re, a_im, log_dt, b_re, b_im, c_re, c_im, d, w_glu, b_glu):
    f32 = jnp.float32
    bsz, t, _ = u.shape
    uf = u.astype(f32)
    ug = uf.reshape(bsz, t, SSM_GROUPS, SSM_GROUP_CH).astype(jnp.complex64)
    a = lax.complex(a_re.astype(f32), a_im.astype(f32))
    dt = jnp.exp(log_dt.astype(f32))[:, None]
    a_bar = jnp.exp(dt * a)
    b_bar = ((a_bar - 1.0) / a)[..., None] * lax.complex(b_re.astype(f32), b_im.astype(f32))
    c = lax.complex(c_re.astype(f32), c_im.astype(f32))
    bu = jnp.einsum('gnc,btgc->btgn', b_bar, ug)
    bu = bu.at[:, 0].add(a_bar * h0)
    a_seq = jnp.broadcast_to(a_bar, bu.shape)

    def combine(e1, e2):
        a1, b1 = e1
        a2, b2 = e2
        return a1 * a2, a2 * b1 + b2

    _, h = lax.associative_scan(combine, (a_seq, bu), axis=1)
    y = jnp.einsum('gcn,btgn->btgc', c, h).real.reshape(bsz, t, SSM_WIDTH) + d.astype(f32) * uf
    yg = jax.nn.gelu(y)
    out = yg * jax.nn.sigmoid(yg @ w_glu.astype(f32) + b_glu.astype(f32))
    return out.astype(u.dtype), h[:, -1]


def short_conv(x_new, buf, w):
    t = x_new.shape[1]
    cat = jnp.concatenate([buf.astype(x_new.dtype), x_new], axis=1)
    y = sum(cat[:, j:j + t] * w[:, j] for j in range(CONV_W))
    return jax.nn.silu(y), cat[:, -(CONV_W - 1):]


def gated_delta_rule(q, k, v, g, beta, s0):
    bsz, t, nh, _ = q.shape
    n_chunks = -(-t // DN_CHUNK)
    pad = n_chunks * DN_CHUNK - t

    def prep4(x):
        x = jnp.pad(x, [(0, 0), (0, pad), (0, 0), (0, 0)])
        return x.reshape(bsz, n_chunks, DN_CHUNK, nh, x.shape[-1]).transpose(1, 0, 3, 2, 4)

    def prep3(x):
        x = jnp.pad(x, [(0, 0), (0, pad), (0, 0)])
        return x.reshape(bsz, n_chunks, DN_CHUNK, nh).transpose(1, 0, 3, 2)

    idx = jnp.arange(DN_CHUNK)
    incl = idx[:, None] >= idx[None, :]
    strict = idx[:, None] > idx[None, :]
    eye = jnp.eye(DN_CHUNK, dtype=jnp.float32)

    def step(s, inp):
        qc, kc, vc, gc, bc = inp
        cum = jnp.cumsum(gc, axis=-1)
        gam = jnp.exp(cum)
        diff = cum[..., :, None] - cum[..., None, :]
        decay = jnp.where(incl, jnp.exp(jnp.where(incl, diff, 0.0)), 0.0)
        kk = jnp.einsum('bhik,bhjk->bhij', kc, kc)
        a_mat = jnp.where(strict, bc[..., :, None] * decay * kk, 0.0)
        rhs = jnp.concatenate([bc[..., None] * vc, (bc * gam)[..., None] * kc], axis=-1)
        sol = lax.linalg.triangular_solve(eye + a_mat, rhs, left_side=True, lower=True)
        w = sol[..., :DN_DV] - jnp.einsum('bhck,bhkv->bhcv', sol[..., DN_DV:], s)
        qk = jnp.einsum('bhik,bhjk->bhij', qc, kc) * decay
        o = gam[..., None] * jnp.einsum('bhck,bhkv->bhcv', qc, s) + jnp.einsum('bhij,bhjv->bhiv', qk, w)
        k_dec = kc * jnp.exp(cum[..., -1:] - cum)[..., None]
        s_new = jnp.exp(cum[..., -1])[..., None, None] * s + jnp.einsum('bhck,bhcv->bhkv', k_dec, w)
        return s_new, o

    s_t, o = lax.scan(step, s0, (prep4(q), prep4(k), prep4(v), prep3(g), prep3(beta)))
    o = o.transpose(1, 0, 3, 2, 4).reshape(bsz, n_chunks * DN_CHUNK, nh, DN_DV)[:, :t]
    return o, s_t


def deltanet_branch(qkv, a, b, z, conv_buf, s0, conv_w, a_log, dt_bias, norm_g):
    f32 = jnp.float32
    bsz, t, _ = qkv.shape
    y, new_buf = short_conv(qkv, conv_buf, conv_w)
    y = y.astype(f32)
    q, k, v = jnp.split(y, [DN_HEADS * DN_DK, 2 * DN_HEADS * DN_DK], axis=-1)
    q = l2_normalize(q.reshape(bsz, t, DN_HEADS, DN_DK)) * (DN_DK ** -0.5)
    k = l2_normalize(k.reshape(bsz, t, DN_HEADS, DN_DK))
    v = v.reshape(bsz, t, DN_HEADS, DN_DV)
    g = -jnp.exp(a_log.astype(f32)) * jax.nn.softplus(a.astype(f32) + dt_bias.astype(f32))
    beta = jax.nn.sigmoid(b.astype(f32))
    o, s_t = gated_delta_rule(q, k, v, g, beta, s0.astype(f32))
    o = rms_norm(o, norm_g) * jax.nn.silu(z.astype(f32)).reshape(bsz, t, DN_HEADS, DN_DV)
    return o.reshape(bsz, t, DN_WIDTH).astype(qkv.dtype), new_buf, s_t


def diff_attend(q, k, v, lam, q_offset):
    f32 = jnp.float32
    qf, kf = q.astype(f32), k.astype(f32)
    scale = DA_DH ** -0.5
    s1 = jnp.einsum('bqhd,bkhd->bhqk', qf[..., :DA_DH], kf[..., :DA_DH]) * scale
    s2 = jnp.einsum('bqhd,bkhd->bhqk', qf[..., DA_DH:], kf[..., DA_DH:]) * scale
    tq, tk = q.shape[1], k.shape[1]
    mask = jnp.arange(tk)[None, :] <= (q_offset + jnp.arange(tq))[:, None]
    p1 = jax.nn.softmax(jnp.where(mask, s1, -jnp.inf), axis=-1)
    p2 = jax.nn.softmax(jnp.where(mask, s2, -jnp.inf), axis=-1)
    return jnp.einsum('bhqk,bkhv->bqhv', p1 - lam * p2, v.astype(f32))


def diff_attention_branch(qkv, past_k, past_v, lam_vecs, lam_init, norm_g):
    f32 = jnp.float32
    bsz, t, _ = qkv.shape
    q, k, v = jnp.split(qkv, 3, axis=-1)
    q = q.reshape(bsz, t, DA_HEADS, 2 * DA_DH)
    k = k.reshape(bsz, t, DA_HEADS, 2 * DA_DH)
    v = v.reshape(bsz, t, DA_HEADS, DA_DV)
    lv = lam_vecs.astype(f32)
    lam = jnp.exp(jnp.sum(lv[0] * lv[1])) - jnp.exp(jnp.sum(lv[2] * lv[3])) + lam_init
    if past_k is None:
        outs = [diff_attend(q[:, s:s + Q_BLOCK], k[:, :s + Q_BLOCK], v[:, :s + Q_BLOCK], lam, s)
                for s in range(0, t, Q_BLOCK)]
        o = jnp.concatenate(outs, axis=1)
    else:
        n_past = past_k.shape[1] * past_k.shape[2]
        keys = jnp.concatenate([past_k.reshape(bsz, n_past, DA_HEADS, 2 * DA_DH).astype(k.dtype), k], axis=1)
        vals = jnp.concatenate([past_v.reshape(bsz, n_past, DA_HEADS, DA_DV).astype(v.dtype), v], axis=1)
        o = diff_attend(q, keys, vals, lam, n_past)
    o = rms_norm(o, norm_g) * (1.0 - lam_init)
    return o.reshape(bsz, t, DA_WIDTH).astype(qkv.dtype), k, v


def moe_ffn(x, w_router, b_router, w_gate_e, w_up_e, w_down_e):
    f32 = jnp.float32
    scores = jax.nn.softmax(jnp.einsum('btd,de->bte', x, w_router).astype(f32), axis=-1)
    sel = scores + b_router.astype(f32)
    sel_g = sel.reshape(*sel.shape[:-1], N_EXPERT_GROUPS, EXPERTS_PER_GROUP)
    grp_score = jnp.sum(lax.top_k(sel_g, TOP_K)[0], axis=-1)
    grp = jnp.argmax(grp_score, axis=-1)
    gidx = jnp.broadcast_to(grp[..., None, None], (*grp.shape, 1, EXPERTS_PER_GROUP))
    sel_in = jnp.take_along_axis(sel_g, gidx, axis=-2)[..., 0, :]
    _, idx_in = lax.top_k(sel_in, TOP_K)
    idx = grp[..., None] * EXPERTS_PER_GROUP + idx_in
    w = jnp.take_along_axis(scores, idx, axis=-1)
    w = w / jnp.sum(w, axis=-1, keepdims=True)
    gate = jnp.sum(jax.nn.one_hot(idx, N_EXPERTS, dtype=f32) * w[..., None], axis=-2)
    hg = jnp.einsum('btd,edf->btef', x, w_gate_e)
    hu = jnp.einsum('btd,edf->btef', x, w_up_e)
    h = jax.nn.silu(hg) * hu * gate[..., None].astype(x.dtype)
    return jnp.einsum('btef,efd->btd', h, w_down_e)


def trunk_layer(x, lp, lam_init, w_router, b_router, ssm_h0, conv_buf, delta_s0, past_k, past_v):
    f32 = jnp.float32
    bsz, t, _ = x.shape
    u = jnp.einsum('btd,dn->btn', x, lp['w_in'])
    u_ssm, qkv_dn, a_dn, b_dn, z_dn, qkv_da, g_logit = jnp.split(u, _in_splits(), axis=-1)
    y_a, ssm_h = ssm_branch(u_ssm, ssm_h0, lp['ssm_a_re'], lp['ssm_a_im'], lp['ssm_log_dt'],
                            lp['ssm_b_re'], lp['ssm_b_im'], lp['ssm_c_re'], lp['ssm_c_im'],
                            lp['ssm_d'], lp['ssm_w_glu'], lp['ssm_b_glu'])
    y_b, conv_new, delta_s = deltanet_branch(qkv_dn, a_dn, b_dn, z_dn, conv_buf, delta_s0, lp['dn_conv_w'],
                                             lp['dn_a_log'], lp['dn_dt_bias'], lp['dn_norm_g'])
    y_c, k_new, v_new = diff_attention_branch(qkv_da, past_k, past_v, lp['da_lambda'], lam_init, lp['da_norm_g'])
    gates = jax.nn.sigmoid((g_logit + lp['b_gate']).astype(f32)).reshape(bsz, t, N_BRANCH, D_MODEL)
    branches = jnp.einsum('btgw,gwd->btgd', jnp.stack([y_a, y_b, y_c], axis=2), lp['w_branch'])
    merged = jnp.sum(gates * branches.astype(f32), axis=2).astype(x.dtype)
    mix = jnp.einsum('btd,de->bte', merged, lp['w_out'])
    x = layer_norm(DEEPNORM_ALPHA * x + mix, lp['ln_g'][0], lp['ln_b'][0])
    ffn = moe_ffn(x, w_router, b_router, lp['w_gate_e'], lp['w_up_e'], lp['w_down_e'])
    x = layer_norm(DEEPNORM_ALPHA * x + ffn, lp['ln_g'][1], lp['ln_b'][1])
    return x, ssm_h, conv_new, delta_s, k_new, v_new


def setup_inputs(seed: int = 0) -> dict:
    key = jax.random.key(seed)
    ks = jax.random.split(key, 40)
    f32 = jnp.float32

    def nrm(i, shape, scale):
        return jax.random.normal(ks[i], shape, f32) * scale

    n_pages = PAST_LEN // PAGE_SIZE
    n_used = DEC_BATCH * n_pages
    n_pool = (n_used * 5) // 4
    page_table = jax.random.permutation(ks[0], n_pool)[:n_used].reshape(DEC_BATCH, n_pages).astype(jnp.int32)
    gshape = (DEPTH, SSM_GROUPS, SSM_STATE)
    dt_dn = jnp.exp(jax.random.uniform(ks[23], (DEPTH, DN_HEADS), f32, math.log(1e-3), math.log(1e-1)))
    return {
        'x_prompt': nrm(1, (BATCH, SEQ, D_MODEL), 1.0),
        'x_sample': nrm(2, (DEC_BATCH, DEC_SEQ, D_MODEL), 1.0),
        'state_ssm_re': nrm(3, (DEPTH, DEC_BATCH, SSM_GROUPS, SSM_STATE), 0.1),
        'state_ssm_im': nrm(4, (DEPTH, DEC_BATCH, SSM_GROUPS, SSM_STATE), 0.1),
        'state_conv': nrm(5, (DEPTH, DEC_BATCH, CONV_W - 1, DN_QKV), 1.0),
        'state_delta': nrm(6, (DEPTH, DEC_BATCH, DN_HEADS, DN_DK, DN_DV), 0.1),
        'cache_k': nrm(7, (DEPTH, n_pool, PAGE_SIZE, DA_HEADS, 2 * DA_DH), 1.0),
        'cache_v': nrm(8, (DEPTH, n_pool, PAGE_SIZE, DA_HEADS, DA_DV), 1.0),
        'page_table': page_table,
        'w_in': nrm(9, (DEPTH, D_MODEL, N_IN), D_MODEL ** -0.5),
        'b_gate': nrm(10, (DEPTH, N_BRANCH * D_MODEL), 0.02),
        'ssm_a_re': -0.5 + nrm(11, gshape, 0.01),
        'ssm_a_im': math.pi * jnp.arange(SSM_STATE, dtype=f32) + nrm(12, gshape, 0.01),
        'ssm_log_dt': jax.random.uniform(ks[13], (DEPTH, SSM_GROUPS), f32, math.log(1e-3), math.log(1e-1)),
        'ssm_b_re': nrm(14, (DEPTH, SSM_GROUPS, SSM_STATE, SSM_GROUP_CH), (2 * SSM_GROUP_CH) ** -0.5),
        'ssm_b_im': nrm(15, (DEPTH, SSM_GROUPS, SSM_STATE, SSM_GROUP_CH), (2 * SSM_GROUP_CH) ** -0.5),
        'ssm_c_re': nrm(16, (DEPTH, SSM_GROUPS, SSM_GROUP_CH, SSM_STATE), SSM_STATE ** -0.5),
        'ssm_c_im': nrm(17, (DEPTH, SSM_GROUPS, SSM_GROUP_CH, SSM_STATE), SSM_STATE ** -0.5),
        'ssm_d': nrm(18, (DEPTH, SSM_WIDTH), 1.0),
        'ssm_w_glu': nrm(19, (DEPTH, SSM_WIDTH, SSM_WIDTH), SSM_WIDTH ** -0.5),
        'ssm_b_glu': nrm(20, (DEPTH, SSM_WIDTH), 0.02),
        'dn_conv_w': nrm(21, (DEPTH, DN_QKV, CONV_W), CONV_W ** -0.5),
        'dn_a_log': jnp.log(jax.random.uniform(ks[22], (DEPTH, DN_HEADS), f32, 1.0, 16.0)),
        'dn_dt_bias': jnp.log(jnp.expm1(dt_dn)),
        'dn_norm_g': 1.0 + nrm(24, (DEPTH, DN_DV), 0.02),
        'da_lambda': nrm(25, (DEPTH, 4, DA_DH), 0.1),
        'da_norm_g': 1.0 + nrm(26, (DEPTH, DA_DV), 0.02),
        'w_branch': nrm(27, (DEPTH, N_BRANCH, MIX_WIDTH, D_MODEL), MIX_WIDTH ** -0.5),
        'w_out': nrm(28, (DEPTH, D_MODEL, D_MODEL), D_MODEL ** -0.5 * DEEPNORM_BETA),
        'ln_g': 1.0 + nrm(29, (DEPTH, 2, D_MODEL), 0.02),
        'ln_b': nrm(30, (DEPTH, 2, D_MODEL), 0.02),
        'w_router': nrm(31, (D_MODEL, N_EXPERTS), D_MODEL ** -0.5),
        'b_router': nrm(32, (N_EXPERTS,), 0.01),
        'w_gate_e': nrm(33, (DEPTH, N_EXPERTS, D_MODEL, D_EXPERT), D_MODEL ** -0.5),
        'w_up_e': nrm(34, (DEPTH, N_EXPERTS, D_MODEL, D_EXPERT), D_MODEL ** -0.5),
        'w_down_e': nrm(35, (DEPTH, N_EXPERTS, D_EXPERT, D_MODEL), D_EXPERT ** -0.5 * DEEPNORM_BETA),
    }


def reference(x_prompt, x_sample, state_ssm_re, state_ssm_im, state_conv, state_delta, cache_k, cache_v,
              page_table, w_in, b_gate, ssm_a_re, ssm_a_im, ssm_log_dt, ssm_b_re, ssm_b_im, ssm_c_re, ssm_c_im,
              ssm_d, ssm_w_glu, ssm_b_glu, dn_conv_w, dn_a_log, dn_dt_bias, dn_norm_g, da_lambda, da_norm_g,
              w_branch, w_out, ln_g, ln_b, w_router, b_router, w_gate_e, w_up_e, w_down_e):
    f32 = jnp.float32
    bp = x_prompt.shape[0]
    ssm_zero = jnp.zeros((bp, SSM_GROUPS, SSM_STATE), jnp.complex64)
    conv_zero = jnp.zeros((bp, CONV_W - 1, DN_QKV), state_conv.dtype)
    delta_zero = jnp.zeros((bp, DN_HEADS, DN_DK, DN_DV), state_delta.dtype)
    xp, xs = x_prompt, x_sample
    p_ssm, p_conv, p_delta, p_k, p_v = [], [], [], [], []
    s_ssm, s_conv, s_delta, s_k, s_v = [], [], [], [], []
    for l in range(DEPTH):
        lp = dict(w_in=w_in[l], b_gate=b_gate[l], ssm_a_re=ssm_a_re[l], ssm_a_im=ssm_a_im[l],
                  ssm_log_dt=ssm_log_dt[l], ssm_b_re=ssm_b_re[l], ssm_b_im=ssm_b_im[l], ssm_c_re=ssm_c_re[l],
                  ssm_c_im=ssm_c_im[l], ssm_d=ssm_d[l], ssm_w_glu=ssm_w_glu[l], ssm_b_glu=ssm_b_glu[l],
                  dn_conv_w=dn_conv_w[l], dn_a_log=dn_a_log[l], dn_dt_bias=dn_dt_bias[l], dn_norm_g=dn_norm_g[l],
                  da_lambda=da_lambda[l], da_norm_g=da_norm_g[l], w_branch=w_branch[l], w_out=w_out[l],
                  ln_g=ln_g[l], ln_b=ln_b[l], w_gate_e=w_gate_e[l], w_up_e=w_up_e[l], w_down_e=w_down_e[l])
        lam_init = 0.8 - 0.6 * math.exp(-0.3 * l)
        xp, h, cb, sd, kn, vn = trunk_layer(xp, lp, lam_init, w_router, b_router,
                                            ssm_zero, conv_zero, delta_zero, None, None)
        p_ssm.append(h); p_conv.append(cb); p_delta.append(sd); p_k.append(kn); p_v.append(vn)
        h0 = lax.complex(state_ssm_re[l].astype(f32), state_ssm_im[l].astype(f32))
        xs, h, cb, sd, kn, vn = trunk_layer(xs, lp, lam_init, w_router, b_router,
                                            h0, state_conv[l], state_delta[l],
                                            cache_k[l, page_table], cache_v[l, page_table])
        s_ssm.append(h); s_conv.append(cb); s_delta.append(sd); s_k.append(kn); s_v.append(vn)
    p_ssm = jnp.stack(p_ssm, axis=0)
    s_ssm = jnp.stack(s_ssm, axis=0)
    return (xp, xs,
            p_ssm.real.astype(state_ssm_re.dtype), p_ssm.imag.astype(state_ssm_im.dtype),
            jnp.stack(p_conv, axis=0).astype(state_conv.dtype), jnp.stack(p_delta, axis=0).astype(state_delta.dtype),
            jnp.stack(p_k, axis=0).astype(cache_k.dtype), jnp.stack(p_v, axis=0).astype(cache_v.dtype),
            s_ssm.real.astype(state_ssm_re.dtype), s_ssm.imag.astype(state_ssm_im.dtype),
            jnp.stack(s_conv, axis=0).astype(state_conv.dtype), jnp.stack(s_delta, axis=0).astype(state_delta.dtype),
            jnp.stack(s_k, axis=0).astype(cache_k.dtype), jnp.stack(s_v, axis=0).astype(cache_v.dtype))
```

```python
import functools
import math

import jax
import jax.numpy as jnp
from jax import lax
from jax.experimental import pallas as pl
from jax.experimental.pallas import tpu as pltpu

F32 = jnp.float32
BF16 = jnp.bfloat16
SDS = jax.ShapeDtypeStruct

D_MODEL = 1024
DEPTH = 4
MIX_WIDTH = 512
N_BRANCH = 3
SSM_GROUP_CH = 16
SSM_GROUPS = 32
SSM_STATE = 64
SSM_CHUNK = 16
DN_DK = 128
DN_HEADS = 4
DN_QKV = 1536
CONV_W = 4
DN_CHUNK = 64
DN_BLOCK = 256
DA_DH = 64
DA_HEADS = 4
N_EXPERTS = 16
EXPERTS_PER_GROUP = 4
D_EXPERT = 512
PAGE_SIZE = 128
DEEPNORM_ALPHA = (2 * DEPTH) ** 0.25
LN_EPS = 1e-5
RMS_EPS = 1e-6
LANES = 128
MIB = 1024 * 1024

COL_DN = 0
COL_DAQ, COL_DAK, COL_DAV = 3, 4, 5
COL_SSM = 6
COL_Z = 7
N_MAIN = 8 * 512


def _cparams(sem, vmem_mib=48):
    return pltpu.CompilerParams(dimension_semantics=sem, vmem_limit_bytes=vmem_mib * MIB)


NN = (((1,), (0,)), ((), ()))
NT = (((1,), (1,)), ((), ()))
TN = (((0,), (0,)), ((), ()))


def _mm1(a, b, dims=NN):
    return lax.dot_general(a.astype(BF16), b.astype(BF16), dims, preferred_element_type=F32)


def _split(a):
    hi = a.astype(BF16)
    return hi, (a - hi.astype(F32)).astype(BF16)


def _mm3(a, b, dims=NN):
    ah, al = a if isinstance(a, tuple) else _split(a)
    bh, bl = b if isinstance(b, tuple) else _split(b)
    d = functools.partial(lax.dot_general, dimension_numbers=dims, preferred_element_type=F32)
    return d(ah, bh) + (d(al, bh) + d(ah, bl))


def _split_param(w):
    w = w.astype(F32)
    hi = lax.reduce_precision(w, exponent_bits=8, mantissa_bits=7)
    return hi.astype(BF16), (w - hi).astype(BF16)


def _split_hbm(w):
    return jnp.stack(_split_param(w), axis=1)


def _dot_hi(a, b):
    return jnp.dot(a, b, preferred_element_type=F32, precision=lax.Precision.HIGHEST)


_mm_ssm = _mm1
_mm_dn = _mm1
_mm_da = _mm1
_mm_moe = _mm1


def _layer_norm(x, g, b):
    mu = jnp.mean(x, axis=-1, keepdims=True)
    xc = x - mu
    var = jnp.mean(xc * xc, axis=-1, keepdims=True)
    return xc * lax.rsqrt(var + LN_EPS) * g + b


def _rms_norm(x, g):
    return x * lax.rsqrt(jnp.mean(x * x, axis=-1, keepdims=True) + RMS_EPS) * g


def _proj_body(x_ref, w_ref, wab_ref, u_ref, ab_ref, xh_ref, xl_ref):
    @pl.when(pl.program_id(1) == 0)
    def _():
        xh, xl = _split(x_ref[...])
        xh_ref[...] = xh
        xl_ref[...] = xl
        ab_ref[...] = _mm3((xh, xl), (wab_ref[0, 0], wab_ref[0, 1]))

    u_ref[...] = _mm3((xh_ref[...], xl_ref[...]), (w_ref[0, 0], w_ref[0, 1]))


def _project(x, w_main, w_ab, layer, tm):
    m = x.shape[0]
    tn = 512
    return pl.pallas_call(
        _proj_body,
        out_shape=(SDS((m, N_MAIN), F32), SDS((m, LANES), F32)),
        grid=(m // tm, N_MAIN // tn),
        in_specs=[pl.BlockSpec((tm, D_MODEL), lambda i, j: (i, 0)),
                  pl.BlockSpec((1, 2, D_MODEL, tn), lambda i, j: (layer, 0, 0, j)),
                  pl.BlockSpec((1, 2, D_MODEL, LANES), lambda i, j: (layer, 0, 0, 0))],
        out_specs=(pl.BlockSpec((tm, tn), lambda i, j: (i, j)),
                   pl.BlockSpec((tm, LANES), lambda i, j: (i, 0))),
        scratch_shapes=[pltpu.VMEM((tm, D_MODEL), BF16), pltpu.VMEM((tm, D_MODEL), BF16)],
        compiler_params=_cparams(("parallel", "arbitrary")),
        name="proj",
    )(x, w_main, w_ab)


def _ssm_constants(a_re, a_im, log_dt, b_re, b_im, c_re, c_im, d):
    hp = lax.Precision.HIGHEST
    ell = SSM_CHUNK
    a = lax.complex(a_re.astype(F32), a_im.astype(F32))
    dt = jnp.exp(log_dt.astype(F32))[..., None]
    a_bar = jnp.exp(dt * a)
    b_bar = ((a_bar - 1.0) / a)[..., None] * lax.complex(b_re.astype(F32), b_im.astype(F32))
    c = lax.complex(c_re.astype(F32), c_im.astype(F32))
    steps = jnp.arange(ell + 1, dtype=F32)
    pw = jnp.exp((dt * a)[..., None] * steps)
    dd, g, n, ch = b_bar.shape
    kern = jnp.einsum("dgcn,dgnj,dgnk->dgjck", c, pw[..., :ell], b_bar, precision=hp).real
    s_idx = jnp.arange(ell)[:, None]
    t_idx = jnp.arange(ell)[None, :]
    lag = jnp.clip(t_idx - s_idx, 0, ell - 1)
    toep = jnp.where((t_idx >= s_idx)[None, None, :, :, None, None], kern[:, :, lag], 0.0)
    toep = toep.transpose(0, 1, 2, 5, 3, 4).reshape(dd, g, ell * ch, ell * ch)
    pw_rev = pw[..., :ell][..., ::-1]
    e_c = (pw_rev[..., None] * b_bar[:, :, :, None, :]).transpose(0, 1, 3, 4, 2)
    e_mat = jnp.concatenate([e_c.real, e_c.imag], axis=-1).reshape(dd, g, ell * ch, 2 * n)
    cp = (c[..., None] * pw[:, :, None, :, 1:ell + 1]).transpose(0, 1, 3, 4, 2)
    f_mat = jnp.concatenate([cp.real, -cp.imag], axis=2).reshape(dd, g, 2 * n, ell * ch)
    a_l = pw[..., ell]
    ar = jnp.concatenate([a_l.real, a_l.real], axis=-1)
    ai = jnp.concatenate([-a_l.imag, a_l.imag], axis=-1)
    d_t = jnp.tile(d.astype(F32).reshape(dd, g, 1, ch), (1, 1, ell, 1)).reshape(dd, g, 1, ell * ch)
    eye = jnp.eye(g, dtype=F32)
    bre = jnp.einsum("dgnk,gh->dgkhn", b_bar.real, eye).reshape(dd, g * ch, g * n)
    bim = jnp.einsum("dgnk,gh->dgkhn", b_bar.imag, eye).reshape(dd, g * ch, g * n)
    cre = jnp.einsum("dgcn,gh->dgnhc", c.real, eye).reshape(dd, g * n, g * ch)
    cim = jnp.einsum("dgcn,gh->dgnhc", c.imag, eye).reshape(dd, g * n, g * ch)
    return dict(toep=toep, e=e_mat, f=f_mat, ar=ar, ai=ai, d_t=d_t, bre=bre, bim=bim, cre=cre, cim=cim,
                a1r=a_bar.real.reshape(dd, 1, g * n), a1i=a_bar.imag.reshape(dd, 1, g * n),
                d=d.astype(F32).reshape(dd, 1, g * ch))


def _ssm_state_body(u_ref, e_ref, s_ref):
    s_ref[...] = _mm_ssm(u_ref[0], e_ref[0, 0])


def _ssm_scan_body(s_ref, h0_ref, ar_ref, ai_ref, hp_ref, hf_ref):
    ar = ar_ref[...]
    ai = ai_ref[...]

    def body(c, h):
        hp_ref[c] = h
        return ar * h + ai * pltpu.roll(h, SSM_STATE, 1) + s_ref[c]

    hf_ref[...] = lax.fori_loop(0, s_ref.shape[0], body, h0_ref[...])


def _ssm_out_body(u_ref, t_ref, h_ref, f_ref, d_ref, y_ref):
    u = u_ref[0]
    y_ref[0] = _mm_ssm(u, t_ref[0, 0]) + _mm_ssm(h_ref[...], f_ref[0, 0]) + u * d_ref[0, 0]


def _ssm_prompt(u, sc, layer, bsz, t, h0):
    ell, g, ch = SSM_CHUNK, SSM_GROUPS, SSM_GROUP_CH
    nct = t // ell
    nc = bsz * nct
    w = ell * ch
    ug = u[:, COL_SSM * 512:(COL_SSM + 1) * 512].reshape(bsz, nct, ell, g, ch)
    ug = ug.transpose(3, 0, 1, 2, 4).reshape(g, nc, w)
    s = pl.pallas_call(
        _ssm_state_body,
        out_shape=SDS((nc, g * LANES), F32),
        grid=(g,),
        in_specs=[pl.BlockSpec((1, nc, w), lambda i: (i, 0, 0)),
                  pl.BlockSpec((1, 1, w, LANES), lambda i: (layer, i, 0, 0))],
        out_specs=pl.BlockSpec((nc, LANES), lambda i: (0, i)),
        compiler_params=_cparams(("parallel",)),
        name="ssm_state",
    )(ug, sc["e"])
    s_t = s.reshape(bsz, nct, g, LANES).transpose(1, 0, 2, 3).reshape(nct, bsz * g, LANES)
    rows = bsz * g
    rb = min(64, rows)
    ar = jnp.tile(sc["ar"][layer], (bsz, 1))
    ai = jnp.tile(sc["ai"][layer], (bsz, 1))
    hp, hf = pl.pallas_call(
        _ssm_scan_body,
        out_shape=(SDS((nct, rows, LANES), F32), SDS((rows, LANES), F32)),
        grid=(rows // rb,),
        in_specs=[pl.BlockSpec((nct, rb, LANES), lambda i: (0, i, 0)),
                  pl.BlockSpec((rb, LANES), lambda i: (i, 0)),
                  pl.BlockSpec((rb, LANES), lambda i: (i, 0)),
                  pl.BlockSpec((rb, LANES), lambda i: (i, 0))],
        out_specs=(pl.BlockSpec((nct, rb, LANES), lambda i: (0, i, 0)),
                   pl.BlockSpec((rb, LANES), lambda i: (i, 0))),
        compiler_params=_cparams(("parallel",)),
        name="ssm_scan",
    )(s_t, h0, ar, ai)
    hprev = hp.reshape(nct, bsz, g, LANES).transpose(1, 0, 2, 3).reshape(nc, g * LANES)
    y = pl.pallas_call(
        _ssm_out_body,
        out_shape=SDS((g, nc, w), F32),
        grid=(g,),
        in_specs=[pl.BlockSpec((1, nc, w), lambda i: (i, 0, 0)),
                  pl.BlockSpec((1, 1, w, w), lambda i: (layer, i, 0, 0)),
                  pl.BlockSpec((nc, LANES), lambda i: (0, i)),
                  pl.BlockSpec((1, 1, LANES, w), lambda i: (layer, i, 0, 0)),
                  pl.BlockSpec((1, 1, 1, w), lambda i: (layer, i, 0, 0))],
        out_specs=pl.BlockSpec((1, nc, w), lambda i: (i, 0, 0)),
        compiler_params=_cparams(("parallel",)),
        name="ssm_out",
    )(ug, sc["toep"], hprev, sc["f"], sc["d_t"])
    y = y.reshape(g, bsz, nct, ell, ch).transpose(1, 2, 3, 0, 4).reshape(bsz * t, g * ch)
    hf = hf.reshape(bsz, g, LANES)
    return y, hf[..., :SSM_STATE], hf[..., SSM_STATE:]


def _ssm_step_body(u_ref, hr_ref, hi_ref, ar_ref, ai_ref, bre_ref, bim_ref, cre_ref, cim_ref, d_ref,
                   y_ref, nr_ref, ni_ref):
    u = u_ref[...]
    hr, hi = hr_ref[...], hi_ref[...]
    ar, ai = ar_ref[0], ai_ref[0]
    nr = ar * hr - ai * hi + _mm_ssm(u, bre_ref[0])
    ni = ar * hi + ai * hr + _mm_ssm(u, bim_ref[0])
    nr_ref[...] = nr
    ni_ref[...] = ni
    y_ref[...] = _mm_ssm(nr, cre_ref[0]) - _mm_ssm(ni, cim_ref[0]) + u * d_ref[0]


def _ssm_step(u, sc, layer, h_re, h_im):
    ms = u.shape[0]
    gn = SSM_GROUPS * SSM_STATE
    full = lambda shape: pl.BlockSpec(shape, lambda i: (0,) * len(shape))
    lay = lambda shape: pl.BlockSpec((1,) + shape, lambda i: (layer,) + (0,) * len(shape))
    return pl.pallas_call(
        _ssm_step_body,
        out_shape=(SDS((ms, 512), F32), SDS((ms, gn), F32), SDS((ms, gn), F32)),
        grid=(1,),
        in_specs=[pl.BlockSpec((ms, 512), lambda i: (0, COL_SSM)), full((ms, gn)), full((ms, gn)),
                  lay((1, gn)), lay((1, gn)), lay((512, gn)), lay((512, gn)), lay((gn, 512)), lay((gn, 512)),
                  lay((1, 512))],
        out_specs=(full((ms, 512)), full((ms, gn)), full((ms, gn))),
        compiler_params=_cparams(("arbitrary",)),
        name="ssm_step",
    )(u, h_re, h_im, sc["a1r"], sc["a1i"], sc["bre"], sc["bim"], sc["cre"], sc["cim"], sc["d"])


def _shift_rows(x, prev8, s):
    if s == 0:
        return x
    xs = pltpu.roll(x, s, 0)
    ps = pltpu.roll(prev8, s, 0)
    row = lax.broadcasted_iota(jnp.int32, ps.shape, 0)
    head = jnp.where(row < s, ps, xs[:8])
    return jnp.concatenate([head, xs[8:]], axis=0)


def _dn_prompt_body(x_ref, prev_ref, buf_ref, ab_ref, abt_ref, z_ref, s0_ref, cw_ref, al_ref, dtb_ref,
                    alc_ref, dtbc_ref, ng_ref, y_ref, sf_ref, s_scr):
    i = pl.program_id(1)
    nb = pl.num_programs(1)
    tb = DN_BLOCK
    nck = tb // DN_CHUNK

    @pl.when(i == 0)
    def _():
        s_scr[...] = s0_ref[0]

    x = x_ref[...]
    prev8 = jnp.where(i == 0, buf_ref[0], prev_ref[...])
    cw = cw_ref[0]
    conv = x * cw[3:4]
    for j in range(CONV_W - 1):
        conv = conv + _shift_rows(x, prev8, CONV_W - 1 - j) * cw[j:j + 1]
    y = conv * jax.nn.sigmoid(conv)

    ab = ab_ref[...]
    gmat = -jnp.exp(al_ref[0]) * jax.nn.softplus(ab + dtb_ref[0])
    beta_m = jax.nn.sigmoid(ab)
    abt = abt_ref[0]
    g_rows = -jnp.exp(alc_ref[0]) * jax.nn.softplus(abt + dtbc_ref[0])

    r = lax.broadcasted_iota(jnp.int32, (tb, tb), 0)
    c = lax.broadcasted_iota(jnp.int32, (tb, tb), 1)
    same = (r // DN_CHUNK) == (c // DN_CHUNK)
    incl = same & (r >= c)
    strict = same & (r > c)
    l_incl = jnp.where(incl, 1.0, 0.0).astype(F32)
    u_incl = jnp.where(same & (c >= r), 1.0, 0.0).astype(F32)
    blk1 = jnp.where(same, 1.0, 0.0).astype(F32)
    cum_c = _dot_hi(l_incl, gmat)
    tot_c = _dot_hi(blk1, gmat)
    cum_r = _dot_hi(g_rows, u_incl)
    rowi = lax.broadcasted_iota(jnp.int32, (tb, 1), 0)

    outs = []
    for h in range(DN_HEADS):
        qh = y[:, h * DN_DK:(h + 1) * DN_DK]
        kh = y[:, 512 + h * DN_DK:512 + (h + 1) * DN_DK]
        vh = y[:, 1024 + h * DN_DK:1024 + (h + 1) * DN_DK]
        qh = qh * lax.rsqrt(jnp.sum(qh * qh, axis=-1, keepdims=True) + RMS_EPS) * (DN_DK ** -0.5)
        kh = kh * lax.rsqrt(jnp.sum(kh * kh, axis=-1, keepdims=True) + RMS_EPS)
        cc = cum_c[:, h:h + 1]
        cr = cum_r[h:h + 1, :]
        tot = tot_c[:, h:h + 1]
        beta = beta_m[:, 4 + h:5 + h]
        gam = jnp.exp(cc)
        decay = jnp.where(incl, jnp.exp(jnp.where(incl, cc - cr, 0.0)), 0.0)
        kk = _mm_dn(kh, kh, NT)
        a_mat = jnp.where(strict, beta * decay * kk, 0.0)
        n_mat = -a_mat
        p_mat = a_mat
        for _ in range(5):
            p_mat = _mm_dn(p_mat, p_mat)
            n_mat = n_mat + p_mat + _mm_dn(n_mat, p_mat)
        rhs = jnp.concatenate([beta * vh, (beta * gam) * kh], axis=1)
        sol = rhs + _mm_dn(n_mat, rhs)
        sol_v = sol[:, :DN_DK]
        sol_k = sol[:, DN_DK:]
        qk = jnp.where(incl, _mm_dn(qh, kh, NT) * decay, 0.0)
        qg = qh * gam
        k_dec = kh * jnp.exp(tot - cc)
        e_tot = jnp.exp(tot)
        s = s_scr[h]
        o_inter, ws = [], []
        for ck in range(nck):
            lo, hi = ck * DN_CHUNK, (ck + 1) * DN_CHUNK
            w_c = sol_v[lo:hi] - _mm_dn(sol_k[lo:hi], s)
            o_inter.append(_mm_dn(qg[lo:hi], s))
            s = e_tot[lo:lo + 1] * s + _mm_dn(k_dec[lo:hi], w_c, TN)
            ws.append(w_c)
        s_scr[h] = s
        w_all = jnp.concatenate(ws, axis=0)
        o = jnp.concatenate(o_inter, axis=0) + _mm_dn(qk, w_all)
        zh = z_ref[:, h * DN_DK:(h + 1) * DN_DK]
        outs.append(_rms_norm(o, ng_ref[0]) * (zh * jax.nn.sigmoid(zh)))
    y_ref[...] = jnp.concatenate(outs, axis=1)

    @pl.when(i == nb - 1)
    def _():
        sf_ref[0] = s_scr[...]


def _dn_prompt(u, ab, dc, layer, bsz, t, conv_buf8, s0):
    tb = DN_BLOCK
    nb = t // tb
    abt = ab[:, :8].reshape(bsz, t, 8).transpose(0, 2, 1)
    lay = lambda shape: pl.BlockSpec((1,) + shape, lambda b, i: (layer,) + (0,) * len(shape))
    return pl.pallas_call(
        _dn_prompt_body,
        out_shape=(SDS((bsz * t, 512), F32), SDS((bsz, DN_HEADS, DN_DK, DN_DK), F32)),
        grid=(bsz, nb),
        in_specs=[pl.BlockSpec((tb, DN_QKV), lambda b, i: (b * nb + i, 0)),
                  pl.BlockSpec((8, DN_QKV), lambda b, i: (jnp.maximum((b * nb + i) * (tb // 8) - 1, 0), 0)),
                  pl.BlockSpec((1, 8, DN_QKV), lambda b, i: (b, 0, 0)),
                  pl.BlockSpec((tb, LANES), lambda b, i: (b * nb + i, 0)),
                  pl.BlockSpec((1, 8, tb), lambda b, i: (b, 0, i)),
                  pl.BlockSpec((tb, 512), lambda b, i: (b * nb + i, COL_Z)),
                  pl.BlockSpec((1, DN_HEADS, DN_DK, DN_DK), lambda b, i: (b, 0, 0, 0)),
                  lay((8, DN_QKV)), lay((1, LANES)), lay((1, LANES)), lay((8, tb)), lay((8, tb)),
                  lay((1, LANES))],
        out_specs=(pl.BlockSpec((tb, 512), lambda b, i: (b * nb + i, 0)),
                   pl.BlockSpec((1, DN_HEADS, DN_DK, DN_DK), lambda b, i: (b, 0, 0, 0))),
        scratch_shapes=[pltpu.VMEM((DN_HEADS, DN_DK, DN_DK), F32)],
        compiler_params=_cparams(("parallel", "arbitrary")),
        name="dn_prompt",
    )(u, u, conv_buf8, ab, abt, u, s0, dc["cw"], dc["al"], dc["dtb"], dc["alc"], dc["dtbc"], dc["ng"])


def _dn_step_body(x_ref, b0_ref, b1_ref, b2_ref, ab_ref, z_ref, s0_ref, cw_ref, al_ref, dtb_ref, ng_ref,
                  y_ref, sn_ref):
    cw = cw_ref[0]
    conv = b0_ref[...] * cw[0:1] + b1_ref[...] * cw[1:2] + b2_ref[...] * cw[2:3] + x_ref[...] * cw[3:4]
    y = conv * jax.nn.sigmoid(conv)
    ab = ab_ref[...]
    gam_m = jnp.exp(-jnp.exp(al_ref[0]) * jax.nn.softplus(ab + dtb_ref[0]))
    beta_m = jax.nn.sigmoid(ab)
    nrow = x_ref.shape[0]
    row = lax.broadcasted_iota(jnp.int32, (nrow, DN_DK), 0)
    outs = []
    for h in range(DN_HEADS):
        qh = y[:, h * DN_DK:(h + 1) * DN_DK]
        kh = y[:, 512 + h * DN_DK:512 + (h + 1) * DN_DK]
        vh = y[:, 1024 + h * DN_DK:1024 + (h + 1) * DN_DK]
        qh = qh * lax.rsqrt(jnp.sum(qh * qh, axis=-1, keepdims=True) + RMS_EPS) * (DN_DK ** -0.5)
        kh = kh * lax.rsqrt(jnp.sum(kh * kh, axis=-1, keepdims=True) + RMS_EPS)
        gam = gam_m[:, h:h + 1]
        beta = beta_m[:, 4 + h:5 + h]
        ks = jnp.zeros((nrow, DN_DK), F32)
        qs = jnp.zeros((nrow, DN_DK), F32)
        for n in range(nrow):
            s_n = s0_ref[0, n, h]
            ks = jnp.where(row == n, _mm_dn(kh, s_n), ks)
            qs = jnp.where(row == n, _mm_dn(qh, s_n), qs)
        w = beta * vh - (beta * gam) * ks
        o = gam * qs + jnp.sum(qh * kh, axis=-1, keepdims=True) * w
        for n in range(nrow):
            k_only = jnp.where(row == n, kh, 0.0)
            sn_ref[n, h] = gam[n:n + 1] * s0_ref[0, n, h] + _mm_dn(k_only, w, TN)
        zh = z_ref[:, h * DN_DK:(h + 1) * DN_DK]
        outs.append(_rms_norm(o, ng_ref[0]) * (zh * jax.nn.sigmoid(zh)))
    y_ref[...] = jnp.concatenate(outs, axis=1)


def _dn_step(u, ab, dc, layer, bufs, s0_all):
    ms = u.shape[0]
    nr = 8
    lay = lambda shape: pl.BlockSpec((1,) + shape, lambda i: (layer,) + (0,) * len(shape))
    rows = lambda w: pl.BlockSpec((nr, w), lambda i: (i, 0))
    return pl.pallas_call(
        _dn_step_body,
        out_shape=(SDS((ms, 512), F32), SDS((ms, DN_HEADS, DN_DK, DN_DK), F32)),
        grid=(ms // nr,),
        in_specs=[rows(DN_QKV), rows(DN_QKV), rows(DN_QKV), rows(DN_QKV), rows(LANES),
                  pl.BlockSpec((nr, 512), lambda i: (i, COL_Z)),
                  pl.BlockSpec((1, nr, DN_HEADS, DN_DK, DN_DK), lambda i: (layer, i, 0, 0, 0)),
                  lay((8, DN_QKV)), lay((1, LANES)), lay((1, LANES)), lay((1, LANES))],
        out_specs=(rows(512), pl.BlockSpec((nr, DN_HEADS, DN_DK, DN_DK), lambda i: (i, 0, 0, 0))),
        compiler_params=_cparams(("parallel",)),
        name="dn_step",
    )(u, bufs[0], bufs[1], bufs[2], ab, u, s0_all, dc["cw"], dc["al"], dc["dtb"], dc["ng"])


def _da_prompt_body(q_ref, k_ref, v_ref, lam_ref, ng_ref, o_ref, qm_ref, m_ref, l_ref, acc_ref, *, lam_init):
    qi = pl.program_id(1)
    ki = pl.program_id(2)
    tq = q_ref.shape[0]
    tk = k_ref.shape[0]

    @pl.when(ki == 0)
    def _():
        lane = lax.broadcasted_iota(jnp.int32, (tq, LANES), 1)
        for h in range(DA_HEADS):
            qh = q_ref[:, h * LANES:(h + 1) * LANES] * (DA_DH ** -0.5)
            qm_ref[2 * h] = jnp.where(lane < DA_DH, qh, 0.0)
            qm_ref[2 * h + 1] = jnp.where(lane >= DA_DH, qh, 0.0)
        m_ref[...] = jnp.full(m_ref.shape, -jnp.inf, F32)
        l_ref[...] = jnp.zeros(l_ref.shape, F32)
        acc_ref[...] = jnp.zeros(acc_ref.shape, F32)

    def step(diagonal):
        kb = k_ref[...]
        vb = v_ref[...]
        if diagonal:
            r = lax.broadcasted_iota(jnp.int32, (tq, tk), 0)
            c = lax.broadcasted_iota(jnp.int32, (tq, tk), 1)
            keep = c <= r
        for h in range(DA_HEADS):
            kh = kb[:, h * LANES:(h + 1) * LANES]
            vh = vb[:, h * LANES:(h + 1) * LANES]
            for mp in range(2):
                idx = 2 * h + mp
                s = _mm_da(qm_ref[idx], kh, NT)
                if diagonal:
                    s = jnp.where(keep, s, -jnp.inf)
                m_prev = m_ref[idx]
                m_new = jnp.maximum(m_prev, jnp.max(s, axis=-1, keepdims=True))
                alpha = jnp.exp(m_prev - m_new)
                p = jnp.exp(s - m_new)
                l_ref[idx] = alpha * l_ref[idx] + jnp.sum(p, axis=-1, keepdims=True)
                acc_ref[idx] = alpha * acc_ref[idx] + _mm_da(p, vh)
                m_ref[idx] = m_new

    @pl.when(ki < qi)
    def _():
        step(False)

    @pl.when(ki == qi)
    def _():
        step(True)
        lam = lam_ref[0]
        for h in range(DA_HEADS):
            o = acc_ref[2 * h] / l_ref[2 * h] - lam * (acc_ref[2 * h + 1] / l_ref[2 * h + 1])
            o_ref[:, h * LANES:(h + 1) * LANES] = _rms_norm(o, ng_ref[0]) * (1.0 - lam_init)


def _da_prompt(u, ac, layer, bsz, t, lam_init):
    tq = 512
    nq = t // tq
    lay = lambda shape: pl.BlockSpec((1,) + shape, lambda b, i, j: (layer,) + (0,) * len(shape))
    return pl.pallas_call(
        functools.partial(_da_prompt_body, lam_init=lam_init),
        out_shape=SDS((bsz * t, 512), F32),
        grid=(bsz, nq, nq),
        in_specs=[pl.BlockSpec((tq, 512), lambda b, i, j: (b * nq + i, COL_DAQ)),
                  pl.BlockSpec((tq, 512), lambda b, i, j: (b * nq + jnp.minimum(i, j), COL_DAK)),
                  pl.BlockSpec((tq, 512), lambda b, i, j: (b * nq + jnp.minimum(i, j), COL_DAV)),
                  lay((1, LANES)), lay((1, LANES))],
        out_specs=pl.BlockSpec((tq, 512), lambda b, i, j: (b * nq + i, 0)),
        scratch_shapes=[pltpu.VMEM((2 * DA_HEADS, tq, LANES), F32),
                        pltpu.VMEM((2 * DA_HEADS, tq, 1), F32),
                        pltpu.VMEM((2 * DA_HEADS, tq, 1), F32),
                        pltpu.VMEM((2 * DA_HEADS, tq, LANES), F32)],
        compiler_params=_cparams(("parallel", "parallel", "arbitrary")),
        name="da_prompt",
    )(u, u, u, ac["lam"], ac["ng"])


def _da_step_body(pt_ref, q_ref, kn_ref, vn_ref, lam_ref, ng_ref, *refs, n_pages, lam_init):
    k_pages = refs[:n_pages]
    v_pages = refs[n_pages:2 * n_pages]
    o_ref = refs[2 * n_pages]
    s_scr = refs[2 * n_pages + 1]
    nh = DA_HEADS
    ps = PAGE_SIZE
    row = lax.broadcasted_iota(jnp.int32, (LANES, LANES), 0)
    lane = lax.broadcasted_iota(jnp.int32, (LANES, LANES), 1)
    q = q_ref[0] * (DA_DH ** -0.5)
    qmats = []
    for h in range(nh):
        qh = jnp.broadcast_to(q[:, h * LANES:(h + 1) * LANES], (LANES, LANES))
        pick = ((row == 2 * h) & (lane < DA_DH)) | ((row == 2 * h + 1) & (lane >= DA_DH))
        qmats.append(jnp.where(pick, qh, 0.0).astype(BF16))
    for j in range(n_pages):
        acc = jnp.zeros((ps, LANES), F32)
        for h in range(nh):
            kh = k_pages[j][pl.ds(h, ps, stride=nh), :].astype(BF16)
            acc = acc + _mm1(kh, qmats[h], NT)
        s_scr[j * ps:(j + 1) * ps, :] = acc
    accn = jnp.zeros((8, LANES), F32)
    kn = kn_ref[0]
    for h in range(nh):
        kh = jnp.broadcast_to(kn[:, h * LANES:(h + 1) * LANES], (8, LANES)).astype(BF16)
        accn = accn + _mm1(kh, qmats[h], NT)
    row8 = lax.broadcasted_iota(jnp.int32, (8, LANES), 0)
    s_scr[n_pages * ps:n_pages * ps + 8, :] = jnp.where(row8 == 0, accn, -jnp.inf)

    s = s_scr[...]
    m = jnp.max(s, axis=0, keepdims=True)
    p = jnp.exp(s - m)
    pn = (p / jnp.sum(p, axis=0, keepdims=True)).astype(BF16)
    er = lax.broadcasted_iota(jnp.int32, (LANES, nh * LANES), 0)
    ec = lax.broadcasted_iota(jnp.int32, (LANES, nh * LANES), 1) // LANES
    x1 = jnp.where(er == 2 * ec, 1.0, 0.0).astype(BF16)
    x2 = jnp.where(er == 2 * ec + 1, 1.0, 0.0).astype(BF16)
    lam = lam_ref[0][:, 0:1]
    accs = [jnp.zeros((ps, LANES), F32) for _ in range(nh)]
    for j in range(n_pages):
        pj = pn[j * ps:(j + 1) * ps]
        cj = _mm1(pj, x1) - lam * _mm1(pj, x2)
        for h in range(nh):
            vh = v_pages[j][pl.ds(h, ps, stride=nh), :]
            accs[h] = accs[h] + cj[:, h * LANES:(h + 1) * LANES] * vh
    pj = pn[n_pages * ps:n_pages * ps + 8]
    cn = _mm1(pj, x1) - lam * _mm1(pj, x2)
    vn = vn_ref[0]
    for h in range(nh):
        o = jnp.sum(accs[h], axis=0, keepdims=True)
        o = o + jnp.sum(cn[:, h * LANES:(h + 1) * LANES] * vn[:, h * LANES:(h + 1) * LANES], axis=0, keepdims=True)
        o_ref[0, :, h * LANES:(h + 1) * LANES] = _rms_norm(o, ng_ref[0]) * (1.0 - lam_init)


def _da_step(q3, k3, v3, cache_k4, cache_v4, pt_flat, ac, layer, lam_init):
    ms = q3.shape[0]
    n_pages = pt_flat.shape[0] // ms
    rows = PAGE_SIZE * DA_HEADS
    tok = pl.BlockSpec((1, 1, 512), lambda b, pt: (b, 0, 0))
    lay = pl.BlockSpec((1, 1, LANES), lambda b, pt: (layer, 0, 0))

    def page_spec(j):
        return pl.BlockSpec((None, None, rows, LANES), lambda b, pt: (layer, pt[b * n_pages + j], 0, 0))

    grid_spec = pltpu.PrefetchScalarGridSpec(
        num_scalar_prefetch=1,
        grid=(ms,),
        in_specs=[tok, tok, tok, lay, lay] + [page_spec(j) for j in range(n_pages)]
        + [page_spec(j) for j in range(n_pages)],
        out_specs=tok,
        scratch_shapes=[pltpu.VMEM((n_pages * PAGE_SIZE + 8, LANES), F32)],
    )
    return pl.pallas_call(
        functools.partial(_da_step_body, n_pages=n_pages, lam_init=lam_init),
        out_shape=SDS((ms, 1, 512), F32),
        grid_spec=grid_spec,
        compiler_params=_cparams(("arbitrary",), 56),
        name="da_step",
    )(pt_flat, q3, k3, v3, ac["lam"], ac["ng"], *([cache_k4] * n_pages), *([cache_v4] * n_pages))


def _merge_body(x_ref, ya_ref, yb_ref, yc_ref, wg_ref, bg_ref, wglu_ref, bglu_ref, wbr_ref, wout_ref,
                lng_ref, lnb_ref, o_ref):
    x = x_ref[...]
    xs = _split(x)
    ya = jax.nn.gelu(ya_ref[...])
    ya = ya * jax.nn.sigmoid(_mm3(ya, (wglu_ref[0, 0], wglu_ref[0, 1])) + bglu_ref[0])
    ys = (ya, yb_ref[...], yc_ref[...])
    merged = jnp.zeros(x.shape, F32)
    for i in range(N_BRANCH):
        lo, hi = i * D_MODEL, (i + 1) * D_MODEL
        gate = jax.nn.sigmoid(_mm3(xs, (wg_ref[0, 0, :, lo:hi], wg_ref[0, 1, :, lo:hi])) + bg_ref[0, :, lo:hi])
        merged = merged + gate * _mm3(ys[i], (wbr_ref[0, 0, i], wbr_ref[0, 1, i]))
    mix = _mm3(merged, (wout_ref[0, 0], wout_ref[0, 1]))
    o_ref[...] = _layer_norm(DEEPNORM_ALPHA * x + mix, lng_ref[0, 0:1], lnb_ref[0, 0:1])


def _merge(x, ya, yb, yc, mc, layer, tm):
    m = x.shape[0]
    lay = lambda shape: pl.BlockSpec((1,) + shape, lambda i: (layer,) + (0,) * len(shape))
    wlay = lambda shape: pl.BlockSpec((1, 2) + shape, lambda i: (layer,) + (0,) * (len(shape) + 1),
                                      pipeline_mode=pl.Buffered(1))
    tok = lambda w: pl.BlockSpec((tm, w), lambda i: (i, 0))
    return pl.pallas_call(
        _merge_body,
        out_shape=SDS((m, D_MODEL), F32),
        grid=(m // tm,),
        in_specs=[tok(D_MODEL), tok(512), tok(512), tok(512),
                  wlay((D_MODEL, N_BRANCH * D_MODEL)), lay((1, N_BRANCH * D_MODEL)), wlay((512, 512)), lay((1, 512)),
                  wlay((N_BRANCH, 512, D_MODEL)), wlay((D_MODEL, D_MODEL)), lay((2, D_MODEL)), lay((2, D_MODEL))],
        out_specs=tok(D_MODEL),
        compiler_params=_cparams(("parallel",), 56),
        name="merge",
    )(x, ya, yb, yc, mc["wg"], mc["bg"], mc["wglu"], mc["bglu"], mc["wbr"], mc["wout"], mc["lng"], mc["lnb"])


def _router_gate(logits, b_router):
    lane = lax.broadcasted_iota(jnp.int32, logits.shape, 1)
    valid = lane < N_EXPERTS
    lg = jnp.where(valid, logits, -jnp.inf)
    ex = jnp.exp(lg - jnp.max(lg, axis=-1, keepdims=True))
    scores = ex / jnp.sum(ex, axis=-1, keepdims=True)
    neg = -1e30
    sel = jnp.where(valid, scores + b_router, neg)
    pos = lane % EXPERTS_PER_GROUP
    rank = jnp.zeros(logits.shape, F32)
    for s in range(1, EXPERTS_PER_GROUP):
        lo = pltpu.roll(sel, s, 1)
        hi = pltpu.roll(sel, LANES - s, 1)
        rank = rank + jnp.where((pos >= s) & (lo >= sel), 1.0, 0.0)
        rank = rank + jnp.where((pos + s < EXPERTS_PER_GROUP) & (hi > sel), 1.0, 0.0)
    top2 = rank < 2.0
    msel = jnp.where(top2, sel, 0.0)
    gs = msel
    for s in range(1, EXPERTS_PER_GROUP):
        gs = gs + jnp.where(pos >= s, pltpu.roll(msel, s, 1), 0.0)
        gs = gs + jnp.where(pos + s < EXPERTS_PER_GROUP, pltpu.roll(msel, LANES - s, 1), 0.0)
    gs = jnp.where(valid, gs, neg)
    beaten = jnp.zeros(logits.shape, F32)
    for s in range(EXPERTS_PER_GROUP, N_EXPERTS, EXPERTS_PER_GROUP):
        lo = pltpu.roll(gs, s, 1)
        hi = pltpu.roll(gs, LANES - s, 1)
        beaten = beaten + jnp.where((lane >= s) & (lo >= gs), 1.0, 0.0)
        beaten = beaten + jnp.where((lane + s < N_EXPERTS) & (hi > gs), 1.0, 0.0)
    keep = top2 & (beaten < 0.5) & valid
    w = jnp.where(keep, scores, 0.0)
    return w / jnp.sum(w, axis=-1, keepdims=True)


def _moe_body(x_ref, wr_ref, br_ref, wg_ref, wu_ref, wd_ref, lng_ref, lnb_ref, o_ref, xb_ref, gate_ref, acc_ref):
    e = pl.program_id(1)

    @pl.when(e == 0)
    def _():
        xb_ref[...] = x_ref[...].astype(BF16)
        gate_ref[...] = _router_gate(_mm3(x_ref[...], (wr_ref[0], wr_ref[1])), br_ref[...])
        acc_ref[...] = jnp.zeros(acc_ref.shape, F32)

    xb = xb_ref[...]
    hg = _mm_moe(xb, wg_ref[0, 0])
    hu = _mm_moe(xb, wu_ref[0, 0])
    gate = gate_ref[...]
    lane = lax.broadcasted_iota(jnp.int32, gate.shape, 1)
    gcol = jnp.sum(jnp.where(lane == e, gate, 0.0), axis=-1, keepdims=True)
    hid = hg * jax.nn.sigmoid(hg) * hu * gcol
    acc_ref[...] += _mm_moe(hid, wd_ref[0, 0])

    @pl.when(e == N_EXPERTS - 1)
    def _():
        o_ref[...] = _layer_norm(DEEPNORM_ALPHA * x_ref[...] + acc_ref[...], lng_ref[0, 1:2], lnb_ref[0, 1:2])


def _moe(x, ec, layer, tm):
    m = x.shape[0]
    lay2 = lambda shape: pl.BlockSpec((1,) + shape, lambda i, e: (layer,) + (0,) * len(shape))
    exp = lambda shape: pl.BlockSpec((1, 1) + shape, lambda i, e: (layer, e) + (0,) * len(shape))
    return pl.pallas_call(
        _moe_body,
        out_shape=SDS((m, D_MODEL), F32),
        grid=(m // tm, N_EXPERTS),
        in_specs=[pl.BlockSpec((tm, D_MODEL), lambda i, e: (i, 0)),
                  pl.BlockSpec((2, D_MODEL, LANES), lambda i, e: (0, 0, 0)),
                  pl.BlockSpec((1, LANES), lambda i, e: (0, 0)),
                  exp((D_MODEL, D_EXPERT)), exp((D_MODEL, D_EXPERT)), exp((D_EXPERT, D_MODEL)),
                  lay2((2, D_MODEL)), lay2((2, D_MODEL))],
        out_specs=pl.BlockSpec((tm, D_MODEL), lambda i, e: (i, 0)),
        scratch_shapes=[pltpu.VMEM((tm, D_MODEL), BF16), pltpu.VMEM((tm, LANES), F32),
                        pltpu.VMEM((tm, D_MODEL), F32)],
        compiler_params=_cparams(("parallel", "arbitrary")),
        name="moe",
    )(x, ec["wr"], ec["br"], ec["wg"], ec["wu"], ec["wd"], ec["lng"], ec["lnb"])


def _lane_row(v, width=LANES):
    return jnp.pad(v.astype(F32), ((0, 0), (0, width - v.shape[1])))[:, None, :]


def kernel(x_prompt, x_sample, state_ssm_re, state_ssm_im, state_conv, state_delta, cache_k, cache_v, page_table,
           w_in, b_gate, ssm_a_re, ssm_a_im, ssm_log_dt, ssm_b_re, ssm_b_im, ssm_c_re, ssm_c_im, ssm_d, ssm_w_glu,
           ssm_b_glu, dn_conv_w, dn_a_log, dn_dt_bias, dn_norm_g, da_lambda, da_norm_g, w_branch, w_out, ln_g, ln_b,
           w_router, b_router, w_gate_e, w_up_e, w_down_e):
    bp, t, _ = x_prompt.shape
    ms = x_sample.shape[0]
    depth = w_in.shape[0]
    mp = bp * t

    o_dn, o_a, o_b, o_z, o_da, o_gate = 512, 2048, 2052, 2056, 2568, 4104
    w_main = _split_hbm(jnp.concatenate([w_in[:, :, o_dn:o_a], w_in[:, :, o_da:o_gate], w_in[:, :, :o_dn],
                                         w_in[:, :, o_z:o_da]], axis=2))
    w_ab = _split_hbm(jnp.pad(w_in[:, :, o_a:o_z], ((0, 0), (0, 0), (0, LANES - 8))))
    sc = _ssm_constants(ssm_a_re, ssm_a_im, ssm_log_dt, ssm_b_re, ssm_b_im, ssm_c_re, ssm_c_im, ssm_d)
    al = _lane_row(dn_a_log)
    dtb = _lane_row(dn_dt_bias)
    dc = dict(cw=jnp.pad(dn_conv_w.astype(F32).transpose(0, 2, 1), ((0, 0), (0, 8 - CONV_W), (0, 0))),
              al=al, dtb=dtb,
              alc=jnp.broadcast_to(jnp.pad(dn_a_log.astype(F32), ((0, 0), (0, 4)))[:, :, None], (depth, 8, DN_BLOCK)),
              dtbc=jnp.broadcast_to(jnp.pad(dn_dt_bias.astype(F32), ((0, 0), (0, 4)))[:, :, None],
                                    (depth, 8, DN_BLOCK)),
              ng=dn_norm_g.astype(F32)[:, None, :])
    lam_inits = [0.8 - 0.6 * math.exp(-0.3 * l) for l in range(depth)]
    lv = da_lambda.astype(F32)
    lam = (jnp.exp(jnp.sum(lv[:, 0] * lv[:, 1], axis=-1)) - jnp.exp(jnp.sum(lv[:, 2] * lv[:, 3], axis=-1))
           + jnp.asarray(lam_inits, F32))
    ac = dict(lam=jnp.broadcast_to(lam[:, None, None], (depth, 1, LANES)), ng=da_norm_g.astype(F32)[:, None, :])
    mc = dict(wg=_split_hbm(w_in[:, :, o_gate:]), bg=b_gate.astype(F32)[:, None, :],
              wglu=_split_hbm(ssm_w_glu), bglu=ssm_b_glu.astype(F32)[:, None, :],
              wbr=_split_hbm(w_branch), wout=_split_hbm(w_out), lng=ln_g.astype(F32), lnb=ln_b.astype(F32))
    ec = dict(wr=jnp.stack(_split_param(jnp.pad(w_router, ((0, 0), (0, LANES - N_EXPERTS)))), axis=0),
              br=jnp.pad(b_router.astype(F32), (0, LANES - N_EXPERTS))[None, :],
              wg=w_gate_e.astype(BF16), wu=w_up_e.astype(BF16), wd=w_down_e.astype(BF16),
              lng=mc["lng"], lnb=mc["lnb"])

    n_pool = cache_k.shape[1]
    cache_k4 = cache_k.reshape(depth, n_pool, PAGE_SIZE * DA_HEADS, LANES)
    cache_v4 = cache_v.reshape(depth, n_pool, PAGE_SIZE * DA_HEADS, LANES)
    pt_flat = page_table.reshape(-1).astype(jnp.int32)

    xp = x_prompt.reshape(mp, D_MODEL)
    xs = x_sample.reshape(ms, D_MODEL)
    zero_h = jnp.zeros((bp * SSM_GROUPS, LANES), F32)
    zero_buf = jnp.zeros((bp, 8, DN_QKV), F32)
    zero_s = jnp.zeros((bp, DN_HEADS, DN_DK, DN_DK), F32)
    outs = {k: [] for k in ("p_re", "p_im", "p_conv", "p_delta", "p_k", "p_v",
                            "s_re", "s_im", "s_conv", "s_delta", "s_k", "s_v")}
    for l in range(depth):
        u, ab = _project(xp, w_main, w_ab, l, min(1024, mp))
        ya, h_re, h_im = _ssm_prompt(u, sc, l, bp, t, zero_h)
        yb, s_fin = _dn_prompt(u, ab, dc, l, bp, t, zero_buf, zero_s)
        yc = _da_prompt(u, ac, l, bp, t, lam_inits[l])
        xm = _merge(xp, ya, yb, yc, mc, l, 256)
        xp = _moe(xm, ec, l, 512)
        u3 = u.reshape(bp, t, N_MAIN)
        outs["p_re"].append(h_re)
        outs["p_im"].append(h_im)
        outs["p_conv"].append(u3[:, t - (CONV_W - 1):, :DN_QKV])
        outs["p_delta"].append(s_fin)
        outs["p_k"].append(u3[:, :, COL_DAK * 512:(COL_DAK + 1) * 512].reshape(bp, t, DA_HEADS, LANES))
        outs["p_v"].append(u3[:, :, COL_DAV * 512:(COL_DAV + 1) * 512].reshape(bp, t, DA_HEADS, LANES))
        us, abs_ = _project(xs, w_main, w_ab, l, ms)
        ya_s, n_re, n_im = _ssm_step(us, sc, l, state_ssm_re[l].reshape(ms, -1).astype(F32),
                                     state_ssm_im[l].reshape(ms, -1).astype(F32))
        bufs = [state_conv[l, :, j, :].astype(F32) for j in range(CONV_W - 1)]
        yb_s, s_new = _dn_step(us, abs_, dc, l, bufs, state_delta)
        q3 = us[:, None, COL_DAQ * 512:(COL_DAQ + 1) * 512]
        k3 = us[:, None, COL_DAK * 512:(COL_DAK + 1) * 512]
        v3 = us[:, None, COL_DAV * 512:(COL_DAV + 1) * 512]
        yc_s = _da_step(q3, k3, v3, cache_k4, cache_v4, pt_flat, ac, l, lam_inits[l]).reshape(ms, 512)
        xm_s = _merge(xs, ya_s, yb_s, yc_s, mc, l, ms)
        xs = _moe(xm_s, ec, l, ms)
        outs["s_re"].append(n_re.reshape(ms, SSM_GROUPS, SSM_STATE))
        outs["s_im"].append(n_im.reshape(ms, SSM_GROUPS, SSM_STATE))
        outs["s_conv"].append(jnp.concatenate([state_conv[l, :, 1:, :].astype(F32), us[:, None, :DN_QKV]], axis=1))
        outs["s_delta"].append(s_new)
        outs["s_k"].append(k3.reshape(ms, 1, DA_HEADS, LANES))
        outs["s_v"].append(v3.reshape(ms, 1, DA_HEADS, LANES))

    st = lambda k, dt: jnp.stack(outs[k], axis=0).astype(dt)
    return (xp.reshape(bp, t, D_MODEL), xs.reshape(ms, 1, D_MODEL),
            st("p_re", state_ssm_re.dtype), st("p_im", state_ssm_im.dtype), st("p_conv", state_conv.dtype),
            st("p_delta", state_delta.dtype), st("p_k", cache_k.dtype), st("p_v", cache_v.dtype),
            st("s_re", state_ssm_re.dtype), st("s_im", state_ssm_im.dtype), st("s_conv", state_conv.dtype),
            st("s_delta", state_delta.dtype), st("s_k", cache_k.dtype), st("s_v", cache_v.dtype))
```

```python
import functools
import math

import jax
import jax.numpy as jnp
from jax import lax
from jax.experimental import pallas as pl
from jax.experimental.pallas import tpu as pltpu

F32 = jnp.float32
BF16 = jnp.bfloat16
SDS = jax.ShapeDtypeStruct

D_MODEL = 1024
DEPTH = 4
MIX_WIDTH = 512
N_BRANCH = 3
SSM_GROUP_CH = 16
SSM_GROUPS = 32
SSM_STATE = 64
SSM_CHUNK = 16
DN_DK = 128
DN_HEADS = 4
DN_QKV = 1536
CONV_W = 4
DN_CHUNK = 64
DN_BLOCK = 256
DA_DH = 64
DA_HEADS = 4
N_EXPERTS = 16
EXPERTS_PER_GROUP = 4
D_EXPERT = 512
PAGE_SIZE = 128
DEEPNORM_ALPHA = (2 * DEPTH) ** 0.25
LN_EPS = 1e-5
RMS_EPS = 1e-6
LANES = 128
MIB = 1024 * 1024

COL_DN = 0
COL_DAQ, COL_DAK, COL_DAV = 3, 4, 5
COL_SSM = 6
COL_Z = 7
N_MAIN = 8 * 512


def _cparams(sem, vmem_mib=48):
    return pltpu.CompilerParams(dimension_semantics=sem, vmem_limit_bytes=vmem_mib * MIB)


NN = (((1,), (0,)), ((), ()))
NT = (((1,), (1,)), ((), ()))
TN = (((0,), (0,)), ((), ()))


def _mm1(a, b, dims=NN):
    return lax.dot_general(a.astype(BF16), b.astype(BF16), dims, preferred_element_type=F32)


def _split(a):
    hi = a.astype(BF16)
    return hi, (a - hi.astype(F32)).astype(BF16)


def _mm3(a, b, dims=NN):
    ah, al = a if isinstance(a, tuple) else _split(a)
    bh, bl = b if isinstance(b, tuple) else _split(b)
    d = functools.partial(lax.dot_general, dimension_numbers=dims, preferred_element_type=F32)
    return d(ah, bh) + (d(al, bh) + d(ah, bl))


def _split_body(w_ref, o_ref):
    hi, lo = _split(w_ref[...])
    o_ref[0] = hi
    o_ref[1] = lo


def _split_hbm(w):
    shape = w.shape
    w2 = w.astype(F32).reshape(-1, shape[-1])
    rows, cols = w2.shape
    br = min(rows, 512)
    out = pl.pallas_call(
        _split_body,
        out_shape=SDS((2, rows, cols), BF16),
        grid=(rows // br,),
        in_specs=[pl.BlockSpec((br, cols), lambda i: (i, 0))],
        out_specs=pl.BlockSpec((2, br, cols), lambda i: (0, i, 0)),
        compiler_params=_cparams(("parallel",)),
        name="split",
    )(w2)
    return out.reshape((2,) + shape)


def _dot_hi(a, b):
    return jnp.dot(a, b, preferred_element_type=F32, precision=lax.Precision.HIGHEST)


_mm_ssm = _mm1
_mm_dn = _mm1
_mm_da = _mm1
_mm_moe = _mm1


def _layer_norm(x, g, b):
    mu = jnp.mean(x, axis=-1, keepdims=True)
    xc = x - mu
    var = jnp.mean(xc * xc, axis=-1, keepdims=True)
    return xc * lax.rsqrt(var + LN_EPS) * g + b


def _rms_norm(x, g):
    return x * lax.rsqrt(jnp.mean(x * x, axis=-1, keepdims=True) + RMS_EPS) * g


def _proj_body(x_ref, w_ref, wab_ref, u_ref, ab_ref, xh_ref, xl_ref):
    @pl.when(pl.program_id(1) == 0)
    def _():
        xh, xl = _split(x_ref[...])
        xh_ref[...] = xh
        xl_ref[...] = xl
        ab_ref[...] = _mm3((xh, xl), (wab_ref[0, 0], wab_ref[1, 0]))

    u_ref[...] = _mm3((xh_ref[...], xl_ref[...]), (w_ref[0, 0], w_ref[1, 0]))


def _project(x, w_main, w_ab, layer, tm):
    m = x.shape[0]
    tn = 512
    return pl.pallas_call(
        _proj_body,
        out_shape=(SDS((m, N_MAIN), F32), SDS((m, LANES), F32)),
        grid=(m // tm, N_MAIN // tn),
        in_specs=[pl.BlockSpec((tm, D_MODEL), lambda i, j: (i, 0)),
                  pl.BlockSpec((2, 1, D_MODEL, tn), lambda i, j: (0, layer, 0, j)),
                  pl.BlockSpec((2, 1, D_MODEL, LANES), lambda i, j: (0, layer, 0, 0))],
        out_specs=(pl.BlockSpec((tm, tn), lambda i, j: (i, j)),
                   pl.BlockSpec((tm, LANES), lambda i, j: (i, 0))),
        scratch_shapes=[pltpu.VMEM((tm, D_MODEL), BF16), pltpu.VMEM((tm, D_MODEL), BF16)],
        compiler_params=_cparams(("parallel", "arbitrary")),
        name="proj",
    )(x, w_main, w_ab)


def _kv_out_body(k_ref, v_ref, *refs):
    ko_ref, vo_ref = refs[-2], refs[-1]
    tm = k_ref.shape[0]
    for h in range(DA_HEADS):
        ko_ref[0, pl.ds(h, tm, stride=DA_HEADS), :] = k_ref[:, h * LANES:(h + 1) * LANES]
        vo_ref[0, pl.ds(h, tm, stride=DA_HEADS), :] = v_ref[:, h * LANES:(h + 1) * LANES]


def _kv_out(u, layer, depth, prev):
    m = u.shape[0]
    tm = min(512, m)
    shape = SDS((depth, m * DA_HEADS, LANES), F32)
    in_specs = [pl.BlockSpec((tm, 512), lambda i: (i, COL_DAK)), pl.BlockSpec((tm, 512), lambda i: (i, COL_DAV))]
    args = [u, u]
    aliases = {}
    if prev is not None:
        in_specs += [pl.BlockSpec(memory_space=pl.ANY)] * 2
        args += list(prev)
        aliases = {2: 0, 3: 1}
    out_spec = pl.BlockSpec((1, tm * DA_HEADS, LANES), lambda i: (layer, i, 0))
    return pl.pallas_call(
        _kv_out_body,
        out_shape=(shape, shape),
        grid=(m // tm,),
        in_specs=in_specs,
        out_specs=(out_spec, out_spec),
        input_output_aliases=aliases,
        compiler_params=_cparams(("parallel",)),
        name="kv_out",
    )(*args)


def _ssm_constants(a_re, a_im, log_dt, b_re, b_im, c_re, c_im, d):
    hp = lax.Precision.HIGHEST
    ell = SSM_CHUNK
    a = lax.complex(a_re.astype(F32), a_im.astype(F32))
    dt = jnp.exp(log_dt.astype(F32))[..., None]
    a_bar = jnp.exp(dt * a)
    b_bar = ((a_bar - 1.0) / a)[..., None] * lax.complex(b_re.astype(F32), b_im.astype(F32))
    c = lax.complex(c_re.astype(F32), c_im.astype(F32))
    steps = jnp.arange(ell + 1, dtype=F32)
    pw = jnp.exp((dt * a)[..., None] * steps)
    dd, g, n, ch = b_bar.shape
    kern = jnp.einsum("dgcn,dgnj,dgnk->dgjck", c, pw[..., :ell], b_bar, precision=hp).real
    s_idx = jnp.arange(ell)[:, None, None]
    t_idx = jnp.arange(ell)[None, :, None]
    shift = (t_idx - s_idx == jnp.arange(ell)[None, None, :]).astype(F32)
    toep = jnp.einsum("stj,dgjck->dgsktc", shift, kern, precision=hp).reshape(dd, g, ell * ch, ell * ch)
    pw_rev = pw[..., :ell][..., ::-1]
    e_c = (pw_rev[..., None] * b_bar[:, :, :, None, :]).transpose(0, 1, 3, 4, 2)
    e_mat = jnp.concatenate([e_c.real, e_c.imag], axis=-1).reshape(dd, g, ell * ch, 2 * n)
    cp = (c[..., None] * pw[:, :, None, :, 1:ell + 1]).transpose(0, 1, 3, 4, 2)
    f_mat = jnp.concatenate([cp.real, -cp.imag], axis=2).reshape(dd, g, 2 * n, ell * ch)
    a_l = pw[..., ell]
    ar = jnp.concatenate([a_l.real, a_l.real], axis=-1)
    ai = jnp.concatenate([-a_l.imag, a_l.imag], axis=-1)
    d_t = jnp.tile(d.astype(F32).reshape(dd, g, 1, ch), (1, 1, ell, 1)).reshape(dd, g, 1, ell * ch)
    eye = jnp.eye(g, dtype=F32)
    bre = jnp.einsum("dgnk,gh->dgkhn", b_bar.real, eye).reshape(dd, g * ch, g * n)
    bim = jnp.einsum("dgnk,gh->dgkhn", b_bar.imag, eye).reshape(dd, g * ch, g * n)
    cre = jnp.einsum("dgcn,gh->dgnhc", c.real, eye).reshape(dd, g * n, g * ch)
    cim = jnp.einsum("dgcn,gh->dgnhc", c.imag, eye).reshape(dd, g * n, g * ch)
    return dict(toep=toep, e=e_mat, f=f_mat, ar=ar, ai=ai, d_t=d_t, bre=bre, bim=bim, cre=cre, cim=cim,
                a1r=a_bar.real.reshape(dd, 1, g * n), a1i=a_bar.imag.reshape(dd, 1, g * n),
                d=d.astype(F32).reshape(dd, 1, g * ch))


def _ssm_state_body(u_ref, e_ref, s_ref):
    s_ref[...] = _mm_ssm(u_ref[0], e_ref[0, 0])


def _ssm_scan_body(s_ref, h0_ref, ar_ref, ai_ref, hp_ref, hf_ref):
    ar = ar_ref[...]
    ai = ai_ref[...]

    def body(c, h):
        hp_ref[c] = h
        return ar * h + ai * pltpu.roll(h, SSM_STATE, 1) + s_ref[c]

    hf_ref[...] = lax.fori_loop(0, s_ref.shape[0], body, h0_ref[...])


def _ssm_out_body(u_ref, t_ref, h_ref, f_ref, d_ref, y_ref):
    u = u_ref[0]
    y_ref[0] = _mm_ssm(u, t_ref[0, 0]) + _mm_ssm(h_ref[...], f_ref[0, 0]) + u * d_ref[0, 0]


def _ssm_prompt(u, sc, layer, bsz, t, h0):
    ell, g, ch = SSM_CHUNK, SSM_GROUPS, SSM_GROUP_CH
    nct = t // ell
    nc = bsz * nct
    w = ell * ch
    ug = u[:, COL_SSM * 512:(COL_SSM + 1) * 512].reshape(bsz, nct, ell, g, ch)
    ug = ug.transpose(3, 0, 1, 2, 4).reshape(g, nc, w)
    s = pl.pallas_call(
        _ssm_state_body,
        out_shape=SDS((nc, g * LANES), F32),
        grid=(g,),
        in_specs=[pl.BlockSpec((1, nc, w), lambda i: (i, 0, 0)),
                  pl.BlockSpec((1, 1, w, LANES), lambda i: (layer, i, 0, 0))],
        out_specs=pl.BlockSpec((nc, LANES), lambda i: (0, i)),
        compiler_params=_cparams(("parallel",)),
        name="ssm_state",
    )(ug, sc["e"])
    s_t = s.reshape(bsz, nct, g, LANES).transpose(1, 0, 2, 3).reshape(nct, bsz * g, LANES)
    rows = bsz * g
    rb = min(64, rows)
    ar = jnp.tile(sc["ar"][layer], (bsz, 1))
    ai = jnp.tile(sc["ai"][layer], (bsz, 1))
    hp, hf = pl.pallas_call(
        _ssm_scan_body,
        out_shape=(SDS((nct, rows, LANES), F32), SDS((rows, LANES), F32)),
        grid=(rows // rb,),
        in_specs=[pl.BlockSpec((nct, rb, LANES), lambda i: (0, i, 0)),
                  pl.BlockSpec((rb, LANES), lambda i: (i, 0)),
                  pl.BlockSpec((rb, LANES), lambda i: (i, 0)),
                  pl.BlockSpec((rb, LANES), lambda i: (i, 0))],
        out_specs=(pl.BlockSpec((nct, rb, LANES), lambda i: (0, i, 0)),
                   pl.BlockSpec((rb, LANES), lambda i: (i, 0))),
        compiler_params=_cparams(("parallel",)),
        name="ssm_scan",
    )(s_t, h0, ar, ai)
    hprev = hp.reshape(nct, bsz, g, LANES).transpose(1, 0, 2, 3).reshape(nc, g * LANES)
    y = pl.pallas_call(
        _ssm_out_body,
        out_shape=SDS((g, nc, w), F32),
        grid=(g,),
        in_specs=[pl.BlockSpec((1, nc, w), lambda i: (i, 0, 0)),
                  pl.BlockSpec((1, 1, w, w), lambda i: (layer, i, 0, 0)),
                  pl.BlockSpec((nc, LANES), lambda i: (0, i)),
                  pl.BlockSpec((1, 1, LANES, w), lambda i: (layer, i, 0, 0)),
                  pl.BlockSpec((1, 1, 1, w), lambda i: (layer, i, 0, 0))],
        out_specs=pl.BlockSpec((1, nc, w), lambda i: (i, 0, 0)),
        compiler_params=_cparams(("parallel",)),
        name="ssm_out",
    )(ug, sc["toep"], hprev, sc["f"], sc["d_t"])
    y = y.reshape(g, bsz, nct, ell, ch).transpose(1, 2, 3, 0, 4).reshape(bsz * t, g * ch)
    hf = hf.reshape(bsz, g, LANES)
    return y, hf[..., :SSM_STATE], hf[..., SSM_STATE:]


def _ssm_step_body(u_ref, hr_ref, hi_ref, ar_ref, ai_ref, bre_ref, bim_ref, cre_ref, cim_ref, d_ref,
                   y_ref, nr_ref, ni_ref):
    u = u_ref[...]
    hr, hi = hr_ref[...], hi_ref[...]
    ar, ai = ar_ref[0], ai_ref[0]
    nr = ar * hr - ai * hi + _mm_ssm(u, bre_ref[0])
    ni = ar * hi + ai * hr + _mm_ssm(u, bim_ref[0])
    nr_ref[...] = nr
    ni_ref[...] = ni
    y_ref[...] = _mm_ssm(nr, cre_ref[0]) - _mm_ssm(ni, cim_ref[0]) + u * d_ref[0]


def _ssm_step(u, sc, layer, h_re, h_im):
    ms = u.shape[0]
    gn = SSM_GROUPS * SSM_STATE
    full = lambda shape: pl.BlockSpec(shape, lambda i: (0,) * len(shape))
    lay = lambda shape: pl.BlockSpec((1,) + shape, lambda i: (layer,) + (0,) * len(shape))
    return pl.pallas_call(
        _ssm_step_body,
        out_shape=(SDS((ms, 512), F32), SDS((ms, gn), F32), SDS((ms, gn), F32)),
        grid=(1,),
        in_specs=[pl.BlockSpec((ms, 512), lambda i: (0, COL_SSM)), full((ms, gn)), full((ms, gn)),
                  lay((1, gn)), lay((1, gn)), lay((512, gn)), lay((512, gn)), lay((gn, 512)), lay((gn, 512)),
                  lay((1, 512))],
        out_specs=(full((ms, 512)), full((ms, gn)), full((ms, gn))),
        compiler_params=_cparams(("arbitrary",)),
        name="ssm_step",
    )(u, h_re, h_im, sc["a1r"], sc["a1i"], sc["bre"], sc["bim"], sc["cre"], sc["cim"], sc["d"])


def _shift_rows(x, prev8, s):
    if s == 0:
        return x
    xs = pltpu.roll(x, s, 0)
    ps = pltpu.roll(prev8, s, 0)
    row = lax.broadcasted_iota(jnp.int32, ps.shape, 0)
    head = jnp.where(row < s, ps, xs[:8])
    return jnp.concatenate([head, xs[8:]], axis=0)


def _dn_prompt_body(x_ref, prev_ref, buf_ref, ab_ref, abt_ref, z_ref, s0_ref, cw_ref, al_ref, dtb_ref,
                    alc_ref, dtbc_ref, ng_ref, y_ref, sf_ref, s_scr):
    i = pl.program_id(1)
    nb = pl.num_programs(1)
    tb = DN_BLOCK
    nck = tb // DN_CHUNK

    @pl.when(i == 0)
    def _():
        s_scr[...] = s0_ref[0]

    x = x_ref[...]
    prev8 = jnp.where(i == 0, buf_ref[0], prev_ref[...])
    cw = cw_ref[0]
    conv = x * cw[3:4]
    for j in range(CONV_W - 1):
        conv = conv + _shift_rows(x, prev8, CONV_W - 1 - j) * cw[j:j + 1]
    y = conv * jax.nn.sigmoid(conv)

    ab = ab_ref[...]
    gmat = -jnp.exp(al_ref[0]) * jax.nn.softplus(ab + dtb_ref[0])
    beta_m = jax.nn.sigmoid(ab)
    abt = abt_ref[0]
    g_rows = -jnp.exp(alc_ref[0]) * jax.nn.softplus(abt + dtbc_ref[0])

    r = lax.broadcasted_iota(jnp.int32, (tb, tb), 0)
    c = lax.broadcasted_iota(jnp.int32, (tb, tb), 1)
    same = (r // DN_CHUNK) == (c // DN_CHUNK)
    incl = same & (r >= c)
    strict = same & (r > c)
    l_incl = jnp.where(incl, 1.0, 0.0).astype(F32)
    u_incl = jnp.where(same & (c >= r), 1.0, 0.0).astype(F32)
    blk1 = jnp.where(same, 1.0, 0.0).astype(F32)
    cum_c = _dot_hi(l_incl, gmat)
    tot_c = _dot_hi(blk1, gmat)
    cum_r = _dot_hi(g_rows, u_incl)
    rowi = lax.broadcasted_iota(jnp.int32, (tb, 1), 0)

    outs = []
    for h in range(DN_HEADS):
        qh = y[:, h * DN_DK:(h + 1) * DN_DK]
        kh = y[:, 512 + h * DN_DK:512 + (h + 1) * DN_DK]
        vh = y[:, 1024 + h * DN_DK:1024 + (h + 1) * DN_DK]
        qh = qh * lax.rsqrt(jnp.sum(qh * qh, axis=-1, keepdims=True) + RMS_EPS) * (DN_DK ** -0.5)
        kh = kh * lax.rsqrt(jnp.sum(kh * kh, axis=-1, keepdims=True) + RMS_EPS)
        cc = cum_c[:, h:h + 1]
        cr = cum_r[h:h + 1, :]
        tot = tot_c[:, h:h + 1]
        beta = beta_m[:, 4 + h:5 + h]
        gam = jnp.exp(cc)
        decay = jnp.where(incl, jnp.exp(jnp.where(incl, cc - cr, 0.0)), 0.0)
        kk = _mm_dn(kh, kh, NT)
        a_mat = jnp.where(strict, beta * decay * kk, 0.0)
        n_mat = -a_mat
        p_mat = a_mat
        for _ in range(5):
            p_mat = _mm_dn(p_mat, p_mat)
            n_mat = n_mat + p_mat + _mm_dn(n_mat, p_mat)
        rhs = jnp.concatenate([beta * vh, (beta * gam) * kh], axis=1)
        sol = rhs + _mm_dn(n_mat, rhs)
        sol_v = sol[:, :DN_DK]
        sol_k = sol[:, DN_DK:]
        qk = jnp.where(incl, _mm_dn(qh, kh, NT) * decay, 0.0)
        qg = qh * gam
        k_dec = kh * jnp.exp(tot - cc)
        e_tot = jnp.exp(tot)
        s = s_scr[h]
        o_inter, ws = [], []
        for ck in range(nck):
            lo, hi = ck * DN_CHUNK, (ck + 1) * DN_CHUNK
            w_c = sol_v[lo:hi] - _mm_dn(sol_k[lo:hi], s)
            o_inter.append(_mm_dn(qg[lo:hi], s))
            s = e_tot[lo:lo + 1] * s + _mm_dn(k_dec[lo:hi], w_c, TN)
            ws.append(w_c)
        s_scr[h] = s
        w_all = jnp.concatenate(ws, axis=0)
        o = jnp.concatenate(o_inter, axis=0) + _mm_dn(qk, w_all)
        zh = z_ref[:, h * DN_DK:(h + 1) * DN_DK]
        outs.append(_rms_norm(o, ng_ref[0]) * (zh * jax.nn.sigmoid(zh)))
    y_ref[...] = jnp.concatenate(outs, axis=1)

    @pl.when(i == nb - 1)
    def _():
        sf_ref[0] = s_scr[...]


def _dn_prompt(u, ab, dc, layer, bsz, t, conv_buf8, s0):
    tb = DN_BLOCK
    nb = t // tb
    abt = ab[:, :8].reshape(bsz, t, 8).transpose(0, 2, 1)
    lay = lambda shape: pl.BlockSpec((1,) + shape, lambda b, i: (layer,) + (0,) * len(shape))
    return pl.pallas_call(
        _dn_prompt_body,
        out_shape=(SDS((bsz * t, 512), F32), SDS((bsz, DN_HEADS, DN_DK, DN_DK), F32)),
        grid=(bsz, nb),
        in_specs=[pl.BlockSpec((tb, DN_QKV), lambda b, i: (b * nb + i, 0)),
                  pl.BlockSpec((8, DN_QKV), lambda b, i: (jnp.maximum((b * nb + i) * (tb // 8) - 1, 0), 0)),
                  pl.BlockSpec((1, 8, DN_QKV), lambda b, i: (b, 0, 0)),
                  pl.BlockSpec((tb, LANES), lambda b, i: (b * nb + i, 0)),
                  pl.BlockSpec((1, 8, tb), lambda b, i: (b, 0, i)),
                  pl.BlockSpec((tb, 512), lambda b, i: (b * nb + i, COL_Z)),
                  pl.BlockSpec((1, DN_HEADS, DN_DK, DN_DK), lambda b, i: (b, 0, 0, 0)),
                  lay((8, DN_QKV)), lay((1, LANES)), lay((1, LANES)), lay((8, tb)), lay((8, tb)),
                  lay((1, LANES))],
        out_specs=(pl.BlockSpec((tb, 512), lambda b, i: (b * nb + i, 0)),
                   pl.BlockSpec((1, DN_HEADS, DN_DK, DN_DK), lambda b, i: (b, 0, 0, 0))),
        scratch_shapes=[pltpu.VMEM((DN_HEADS, DN_DK, DN_DK), F32)],
        compiler_params=_cparams(("parallel", "arbitrary")),
        name="dn_prompt",
    )(u, u, conv_buf8, ab, abt, u, s0, dc["cw"], dc["al"], dc["dtb"], dc["alc"], dc["dtbc"], dc["ng"])


def _dn_step_body(x_ref, b0_ref, b1_ref, b2_ref, ab_ref, z_ref, s0_ref, cw_ref, al_ref, dtb_ref, ng_ref,
                  y_ref, sn_ref):
    cw = cw_ref[0]
    conv = b0_ref[...] * cw[0:1] + b1_ref[...] * cw[1:2] + b2_ref[...] * cw[2:3] + x_ref[...] * cw[3:4]
    y = conv * jax.nn.sigmoid(conv)
    ab = ab_ref[...]
    gam_m = jnp.exp(-jnp.exp(al_ref[0]) * jax.nn.softplus(ab + dtb_ref[0]))
    beta_m = jax.nn.sigmoid(ab)
    nrow = x_ref.shape[0]
    row = lax.broadcasted_iota(jnp.int32, (nrow, DN_DK), 0)
    outs = []
    for h in range(DN_HEADS):
        qh = y[:, h * DN_DK:(h + 1) * DN_DK]
        kh = y[:, 512 + h * DN_DK:512 + (h + 1) * DN_DK]
        vh = y[:, 1024 + h * DN_DK:1024 + (h + 1) * DN_DK]
        qh = qh * lax.rsqrt(jnp.sum(qh * qh, axis=-1, keepdims=True) + RMS_EPS) * (DN_DK ** -0.5)
        kh = kh * lax.rsqrt(jnp.sum(kh * kh, axis=-1, keepdims=True) + RMS_EPS)
        gam = gam_m[:, h:h + 1]
        beta = beta_m[:, 4 + h:5 + h]
        ks = jnp.zeros((nrow, DN_DK), F32)
        qs = jnp.zeros((nrow, DN_DK), F32)
        for n in range(nrow):
            s_n = s0_ref[0, n, h]
            ks = jnp.where(row == n, _mm_dn(kh, s_n), ks)
            qs = jnp.where(row == n, _mm_dn(qh, s_n), qs)
        w = beta * vh - (beta * gam) * ks
        o = gam * qs + jnp.sum(qh * kh, axis=-1, keepdims=True) * w
        for n in range(nrow):
            k_only = jnp.where(row == n, kh, 0.0)
            sn_ref[n, h] = gam[n:n + 1] * s0_ref[0, n, h] + _mm_dn(k_only, w, TN)
        zh = z_ref[:, h * DN_DK:(h + 1) * DN_DK]
        outs.append(_rms_norm(o, ng_ref[0]) * (zh * jax.nn.sigmoid(zh)))
    y_ref[...] = jnp.concatenate(outs, axis=1)


def _dn_step(u, ab, dc, layer, bufs, s0_all):
    ms = u.shape[0]
    nr = 8
    lay = lambda shape: pl.BlockSpec((1,) + shape, lambda i: (layer,) + (0,) * len(shape))
    rows = lambda w: pl.BlockSpec((nr, w), lambda i: (i, 0))
    return pl.pallas_call(
        _dn_step_body,
        out_shape=(SDS((ms, 512), F32), SDS((ms, DN_HEADS, DN_DK, DN_DK), F32)),
        grid=(ms // nr,),
        in_specs=[rows(DN_QKV), rows(DN_QKV), rows(DN_QKV), rows(DN_QKV), rows(LANES),
                  pl.BlockSpec((nr, 512), lambda i: (i, COL_Z)),
                  pl.BlockSpec((1, nr, DN_HEADS, DN_DK, DN_DK), lambda i: (layer, i, 0, 0, 0)),
                  lay((8, DN_QKV)), lay((1, LANES)), lay((1, LANES)), lay((1, LANES))],
        out_specs=(rows(512), pl.BlockSpec((nr, DN_HEADS, DN_DK, DN_DK), lambda i: (i, 0, 0, 0))),
        compiler_params=_cparams(("parallel",)),
        name="dn_step",
    )(u, bufs[0], bufs[1], bufs[2], ab, u, s0_all, dc["cw"], dc["al"], dc["dtb"], dc["ng"])


def _da_prompt_body(q_ref, k_ref, v_ref, lam_ref, ng_ref, o_ref, qm_ref, m_ref, l_ref, acc_ref, *, lam_init):
    qi = pl.program_id(1)
    ki = pl.program_id(2)
    tq = q_ref.shape[0]
    tk = k_ref.shape[0]

    @pl.when(ki == 0)
    def _():
        lane = lax.broadcasted_iota(jnp.int32, (tq, LANES), 1)
        for h in range(DA_HEADS):
            qh = q_ref[:, h * LANES:(h + 1) * LANES] * (DA_DH ** -0.5)
            qm_ref[2 * h] = jnp.where(lane < DA_DH, qh, 0.0).astype(BF16)
            qm_ref[2 * h + 1] = jnp.where(lane >= DA_DH, qh, 0.0).astype(BF16)
        m_ref[...] = jnp.full(m_ref.shape, -jnp.inf, F32)
        l_ref[...] = jnp.zeros(l_ref.shape, F32)
        acc_ref[...] = jnp.zeros(acc_ref.shape, F32)

    def step(diagonal):
        kb = k_ref[...].astype(BF16)
        vb = v_ref[...].astype(BF16)
        if diagonal:
            r = lax.broadcasted_iota(jnp.int32, (tk, tq), 0)
            c = lax.broadcasted_iota(jnp.int32, (tk, tq), 1)
            keep = r <= c
        for h in range(DA_HEADS):
            kh = kb[:, h * LANES:(h + 1) * LANES]
            vh = vb[:, h * LANES:(h + 1) * LANES]
            for mp in range(2):
                idx = 2 * h + mp
                s = _mm_da(kh, qm_ref[idx], NT)
                if diagonal:
                    s = jnp.where(keep, s, -jnp.inf)
                m_prev = m_ref[idx:idx + 1, :]
                m_new = jnp.maximum(m_prev, jnp.max(s, axis=0, keepdims=True))
                alpha = jnp.exp(m_prev - m_new)
                p = jnp.exp(s - m_new)
                l_ref[idx:idx + 1, :] = alpha * l_ref[idx:idx + 1, :] + jnp.sum(p, axis=0, keepdims=True)
                acc_ref[idx] = alpha * acc_ref[idx] + _mm_da(vh, p, TN)
                m_ref[idx:idx + 1, :] = m_new

    @pl.when(ki < qi)
    def _():
        step(False)

    @pl.when(ki == qi)
    def _():
        step(True)
        lam = lam_ref[0][:, 0:1]
        for h in range(DA_HEADS):
            o_t = (acc_ref[2 * h] / l_ref[2 * h:2 * h + 1, :]
                   - lam * (acc_ref[2 * h + 1] / l_ref[2 * h + 1:2 * h + 2, :]))
            o_ref[:, h * LANES:(h + 1) * LANES] = _rms_norm(o_t.T, ng_ref[0]) * (1.0 - lam_init)


def _da_prompt(u, ac, layer, bsz, t, lam_init):
    tq = 512
    nq = t // tq
    lay = lambda shape: pl.BlockSpec((1,) + shape, lambda b, i, j: (layer,) + (0,) * len(shape))
    return pl.pallas_call(
        functools.partial(_da_prompt_body, lam_init=lam_init),
        out_shape=SDS((bsz * t, 512), F32),
        grid=(bsz, nq, nq),
        in_specs=[pl.BlockSpec((tq, 512), lambda b, i, j: (b * nq + i, COL_DAQ)),
                  pl.BlockSpec((tq, 512), lambda b, i, j: (b * nq + jnp.minimum(i, j), COL_DAK)),
                  pl.BlockSpec((tq, 512), lambda b, i, j: (b * nq + jnp.minimum(i, j), COL_DAV)),
                  lay((1, LANES)), lay((1, LANES))],
        out_specs=pl.BlockSpec((tq, 512), lambda b, i, j: (b * nq + i, 0)),
        scratch_shapes=[pltpu.VMEM((2 * DA_HEADS, tq, LANES), BF16),
                        pltpu.VMEM((2 * DA_HEADS, tq), F32),
                        pltpu.VMEM((2 * DA_HEADS, tq), F32),
                        pltpu.VMEM((2 * DA_HEADS, LANES, tq), F32)],
        compiler_params=_cparams(("parallel", "parallel", "arbitrary")),
        name="da_prompt",
    )(u, u, u, ac["lam"], ac["ng"])


def _da_step_body(pt_ref, q_ref, kn_ref, vn_ref, lam_ref, ng_ref, *refs, n_pages, lam_init):
    k_pages = refs[:n_pages]
    v_pages = refs[n_pages:2 * n_pages]
    o_ref = refs[2 * n_pages]
    s_scr = refs[2 * n_pages + 1]
    nh = DA_HEADS
    ps = PAGE_SIZE
    row = lax.broadcasted_iota(jnp.int32, (LANES, LANES), 0)
    lane = lax.broadcasted_iota(jnp.int32, (LANES, LANES), 1)
    q = q_ref[0] * (DA_DH ** -0.5)
    qmats = []
    for h in range(nh):
        qh = jnp.broadcast_to(q[:, h * LANES:(h + 1) * LANES], (LANES, LANES))
        pick = ((row == 2 * h) & (lane < DA_DH)) | ((row == 2 * h + 1) & (lane >= DA_DH))
        qmats.append(jnp.where(pick, qh, 0.0).astype(BF16))
    for j in range(n_pages):
        acc = jnp.zeros((ps, LANES), F32)
        for h in range(nh):
            kh = k_pages[j][pl.ds(h, ps, stride=nh), :].astype(BF16)
            acc = acc + _mm1(kh, qmats[h], NT)
        s_scr[j * ps:(j + 1) * ps, :] = acc
    accn = jnp.zeros((8, LANES), F32)
    kn = kn_ref[0]
    for h in range(nh):
        kh = jnp.broadcast_to(kn[:, h * LANES:(h + 1) * LANES], (8, LANES)).astype(BF16)
        accn = accn + _mm1(kh, qmats[h], NT)
    row8 = lax.broadcasted_iota(jnp.int32, (8, LANES), 0)
    s_scr[n_pages * ps:n_pages * ps + 8, :] = jnp.where(row8 == 0, accn, -jnp.inf)

    s = s_scr[...]
    m = jnp.max(s, axis=0, keepdims=True)
    p = jnp.exp(s - m)
    pn = (p / jnp.sum(p, axis=0, keepdims=True)).astype(BF16)
    er = lax.broadcasted_iota(jnp.int32, (LANES, nh * LANES), 0)
    ec = lax.broadcasted_iota(jnp.int32, (LANES, nh * LANES), 1) // LANES
    x1 = jnp.where(er == 2 * ec, 1.0, 0.0).astype(BF16)
    x2 = jnp.where(er == 2 * ec + 1, 1.0, 0.0).astype(BF16)
    lam = lam_ref[0][:, 0:1]
    accs = [jnp.zeros((ps, LANES), F32) for _ in range(nh)]
    for j in range(n_pages):
        pj = pn[j * ps:(j + 1) * ps]
        cj = _mm1(pj, x1) - lam * _mm1(pj, x2)
        for h in range(nh):
            vh = v_pages[j][pl.ds(h, ps, stride=nh), :]
            accs[h] = accs[h] + cj[:, h * LANES:(h + 1) * LANES] * vh
    pj = pn[n_pages * ps:n_pages * ps + 8]
    cn = _mm1(pj, x1) - lam * _mm1(pj, x2)
    vn = vn_ref[0]
    for h in range(nh):
        o = jnp.sum(accs[h], axis=0, keepdims=True)
        o = o + jnp.sum(cn[:, h * LANES:(h + 1) * LANES] * vn[:, h * LANES:(h + 1) * LANES], axis=0, keepdims=True)
        o_ref[0, :, h * LANES:(h + 1) * LANES] = _rms_norm(o, ng_ref[0]) * (1.0 - lam_init)


def _da_step(q3, k3, v3, cache_k4, cache_v4, pt_flat, ac, layer, lam_init):
    ms = q3.shape[0]
    n_pages = pt_flat.shape[0] // ms
    rows = PAGE_SIZE * DA_HEADS
    tok = pl.BlockSpec((1, 1, 512), lambda b, pt: (b, 0, 0))
    lay = pl.BlockSpec((1, 1, LANES), lambda b, pt: (layer, 0, 0))

    def page_spec(j):
        return pl.BlockSpec((None, None, rows, LANES), lambda b, pt: (layer, pt[b * n_pages + j], 0, 0))

    grid_spec = pltpu.PrefetchScalarGridSpec(
        num_scalar_prefetch=1,
        grid=(ms,),
        in_specs=[tok, tok, tok, lay, lay] + [page_spec(j) for j in range(n_pages)]
        + [page_spec(j) for j in range(n_pages)],
        out_specs=tok,
        scratch_shapes=[pltpu.VMEM((n_pages * PAGE_SIZE + 8, LANES), F32)],
    )
    return pl.pallas_call(
        functools.partial(_da_step_body, n_pages=n_pages, lam_init=lam_init),
        out_shape=SDS((ms, 1, 512), F32),
        grid_spec=grid_spec,
        compiler_params=_cparams(("arbitrary",), 56),
        name="da_step",
    )(pt_flat, q3, k3, v3, ac["lam"], ac["ng"], *([cache_k4] * n_pages), *([cache_v4] * n_pages))


def _merge_body(x_ref, ya_ref, yb_ref, yc_ref, wg_ref, bg_ref, wglu_ref, bglu_ref, wbr_ref, wout_ref,
                lng_ref, lnb_ref, o_ref):
    x = x_ref[...]
    xs = _split(x)
    ya = jax.nn.gelu(ya_ref[...])
    ya = ya * jax.nn.sigmoid(_mm3(ya, (wglu_ref[0, 0], wglu_ref[1, 0])) + bglu_ref[0])
    ys = (ya, yb_ref[...], yc_ref[...])
    merged = jnp.zeros(x.shape, F32)
    for i in range(N_BRANCH):
        lo, hi = i * D_MODEL, (i + 1) * D_MODEL
        gate = jax.nn.sigmoid(_mm3(xs, (wg_ref[0, 0, :, lo:hi], wg_ref[1, 0, :, lo:hi])) + bg_ref[0, :, lo:hi])
        merged = merged + gate * _mm3(ys[i], (wbr_ref[0, 0, i], wbr_ref[1, 0, i]))
    mix = _mm3(merged, (wout_ref[0, 0], wout_ref[1, 0]))
    o_ref[...] = _layer_norm(DEEPNORM_ALPHA * x + mix, lng_ref[0, 0:1], lnb_ref[0, 0:1])


def _merge(x, ya, yb, yc, mc, layer, tm):
    m = x.shape[0]
    lay = lambda shape: pl.BlockSpec((1,) + shape, lambda i: (layer,) + (0,) * len(shape))
    wlay = lambda shape: pl.BlockSpec((2, 1) + shape, lambda i: (0, layer) + (0,) * len(shape),
                                      pipeline_mode=pl.Buffered(1))
    tok = lambda w: pl.BlockSpec((tm, w), lambda i: (i, 0))
    return pl.pallas_call(
        _merge_body,
        out_shape=SDS((m, D_MODEL), F32),
        grid=(m // tm,),
        in_specs=[tok(D_MODEL), tok(512), tok(512), tok(512),
                  wlay((D_MODEL, N_BRANCH * D_MODEL)), lay((1, N_BRANCH * D_MODEL)), wlay((512, 512)), lay((1, 512)),
                  wlay((N_BRANCH, 512, D_MODEL)), wlay((D_MODEL, D_MODEL)), lay((2, D_MODEL)), lay((2, D_MODEL))],
        out_specs=tok(D_MODEL),
        compiler_params=_cparams(("parallel",), 56),
        name="merge",
    )(x, ya, yb, yc, mc["wg"], mc["bg"], mc["wglu"], mc["bglu"], mc["wbr"], mc["wout"], mc["lng"], mc["lnb"])


def _router_gate(logits, b_router):
    lane = lax.broadcasted_iota(jnp.int32, logits.shape, 1)
    valid = lane < N_EXPERTS
    lg = jnp.where(valid, logits, -jnp.inf)
    ex = jnp.exp(lg - jnp.max(lg, axis=-1, keepdims=True))
    scores = ex / jnp.sum(ex, axis=-1, keepdims=True)
    neg = -1e30
    sel = jnp.where(valid, scores + b_router, neg)
    pos = lane % EXPERTS_PER_GROUP
    rank = jnp.zeros(logits.shape, F32)
    for s in range(1, EXPERTS_PER_GROUP):
        lo = pltpu.roll(sel, s, 1)
        hi = pltpu.roll(sel, LANES - s, 1)
        rank = rank + jnp.where((pos >= s) & (lo >= sel), 1.0, 0.0)
        rank = rank + jnp.where((pos + s < EXPERTS_PER_GROUP) & (hi > sel), 1.0, 0.0)
    top2 = rank < 2.0
    msel = jnp.where(top2, sel, 0.0)
    gs = msel
    for s in range(1, EXPERTS_PER_GROUP):
        gs = gs + jnp.where(pos >= s, pltpu.roll(msel, s, 1), 0.0)
        gs = gs + jnp.where(pos + s < EXPERTS_PER_GROUP, pltpu.roll(msel, LANES - s, 1), 0.0)
    gs = jnp.where(valid, gs, neg)
    beaten = jnp.zeros(logits.shape, F32)
    for s in range(EXPERTS_PER_GROUP, N_EXPERTS, EXPERTS_PER_GROUP):
        lo = pltpu.roll(gs, s, 1)
        hi = pltpu.roll(gs, LANES - s, 1)
        beaten = beaten + jnp.where((lane >= s) & (lo >= gs), 1.0, 0.0)
        beaten = beaten + jnp.where((lane + s < N_EXPERTS) & (hi > gs), 1.0, 0.0)
    keep = top2 & (beaten < 0.5) & valid
    w = jnp.where(keep, scores, 0.0)
    return w / jnp.sum(w, axis=-1, keepdims=True)


def _moe_body(x_ref, wr_ref, br_ref, wg_ref, wu_ref, wd_ref, lng_ref, lnb_ref, o_ref, xb_ref, gate_ref, acc_ref):
    e = pl.program_id(1)

    @pl.when(e == 0)
    def _():
        xb_ref[...] = x_ref[...].astype(BF16)
        gate_ref[...] = _router_gate(_mm3(x_ref[...], (wr_ref[0], wr_ref[1])), br_ref[...])
        acc_ref[...] = jnp.zeros(acc_ref.shape, F32)

    xb = xb_ref[...]
    hg = _mm_moe(xb, wg_ref[0, 0])
    hu = _mm_moe(xb, wu_ref[0, 0])
    gate = gate_ref[...]
    lane = lax.broadcasted_iota(jnp.int32, gate.shape, 1)
    gcol = jnp.sum(jnp.where(lane == e, gate, 0.0), axis=-1, keepdims=True)
    hid = hg * jax.nn.sigmoid(hg) * hu * gcol
    acc_ref[...] += _mm_moe(hid, wd_ref[0, 0])

    @pl.when(e == N_EXPERTS - 1)
    def _():
        o_ref[...] = _layer_norm(DEEPNORM_ALPHA * x_ref[...] + acc_ref[...], lng_ref[0, 1:2], lnb_ref[0, 1:2])


def _moe(x, ec, layer, tm):
    m = x.shape[0]
    lay2 = lambda shape: pl.BlockSpec((1,) + shape, lambda i, e: (layer,) + (0,) * len(shape))
    exp = lambda shape: pl.BlockSpec((1, 1) + shape, lambda i, e: (layer, e) + (0,) * len(shape))
    return pl.pallas_call(
        _moe_body,
        out_shape=SDS((m, D_MODEL), F32),
        grid=(m // tm, N_EXPERTS),
        in_specs=[pl.BlockSpec((tm, D_MODEL), lambda i, e: (i, 0)),
                  pl.BlockSpec((2, D_MODEL, LANES), lambda i, e: (0, 0, 0)),
                  pl.BlockSpec((1, LANES), lambda i, e: (0, 0)),
                  exp((D_MODEL, D_EXPERT)), exp((D_MODEL, D_EXPERT)), exp((D_EXPERT, D_MODEL)),
                  lay2((2, D_MODEL)), lay2((2, D_MODEL))],
        out_specs=pl.BlockSpec((tm, D_MODEL), lambda i, e: (i, 0)),
        scratch_shapes=[pltpu.VMEM((tm, D_MODEL), BF16), pltpu.VMEM((tm, LANES), F32),
                        pltpu.VMEM((tm, D_MODEL), F32)],
        compiler_params=_cparams(("parallel", "arbitrary")),
        name="moe",
    )(x, ec["wr"], ec["br"], ec["wg"], ec["wu"], ec["wd"], ec["lng"], ec["lnb"])


def _lane_row(v, width=LANES):
    return jnp.pad(v.astype(F32), ((0, 0), (0, width - v.shape[1])))[:, None, :]


def kernel(x_prompt, x_sample, state_ssm_re, state_ssm_im, state_conv, state_delta, cache_k, cache_v, page_table,
           w_in, b_gate, ssm_a_re, ssm_a_im, ssm_log_dt, ssm_b_re, ssm_b_im, ssm_c_re, ssm_c_im, ssm_d, ssm_w_glu,
           ssm_b_glu, dn_conv_w, dn_a_log, dn_dt_bias, dn_norm_g, da_lambda, da_norm_g, w_branch, w_out, ln_g, ln_b,
           w_router, b_router, w_gate_e, w_up_e, w_down_e):
    bp, t, _ = x_prompt.shape
    ms = x_sample.shape[0]
    depth = w_in.shape[0]
    mp = bp * t

    o_dn, o_a, o_b, o_z, o_da, o_gate = 512, 2048, 2052, 2056, 2568, 4104
    w_main = _split_hbm(jnp.concatenate([w_in[:, :, o_dn:o_a], w_in[:, :, o_da:o_gate], w_in[:, :, :o_dn],
                                         w_in[:, :, o_z:o_da]], axis=2))
    w_ab = _split_hbm(jnp.pad(w_in[:, :, o_a:o_z], ((0, 0), (0, 0), (0, LANES - 8))))
    sc = _ssm_constants(ssm_a_re, ssm_a_im, ssm_log_dt, ssm_b_re, ssm_b_im, ssm_c_re, ssm_c_im, ssm_d)
    al = _lane_row(dn_a_log)
    dtb = _lane_row(dn_dt_bias)
    dc = dict(cw=jnp.pad(dn_conv_w.astype(F32).transpose(0, 2, 1), ((0, 0), (0, 8 - CONV_W), (0, 0))),
              al=al, dtb=dtb,
              alc=jnp.broadcast_to(jnp.pad(dn_a_log.astype(F32), ((0, 0), (0, 4)))[:, :, None], (depth, 8, DN_BLOCK)),
              dtbc=jnp.broadcast_to(jnp.pad(dn_dt_bias.astype(F32), ((0, 0), (0, 4)))[:, :, None],
                                    (depth, 8, DN_BLOCK)),
              ng=dn_norm_g.astype(F32)[:, None, :])
    lam_inits = [0.8 - 0.6 * math.exp(-0.3 * l) for l in range(depth)]
    lv = da_lambda.astype(F32)
    lam = (jnp.exp(jnp.sum(lv[:, 0] * lv[:, 1], axis=-1)) - jnp.exp(jnp.sum(lv[:, 2] * lv[:, 3], axis=-1))
           + jnp.asarray(lam_inits, F32))
    ac = dict(lam=jnp.broadcast_to(lam[:, None, None], (depth, 1, LANES)), ng=da_norm_g.astype(F32)[:, None, :])
    mc = dict(wg=_split_hbm(w_in[:, :, o_gate:]), bg=b_gate.astype(F32)[:, None, :],
              wglu=_split_hbm(ssm_w_glu), bglu=ssm_b_glu.astype(F32)[:, None, :],
              wbr=_split_hbm(w_branch), wout=_split_hbm(w_out), lng=ln_g.astype(F32), lnb=ln_b.astype(F32))
    ec = dict(wr=_split_hbm(jnp.pad(w_router, ((0, 0), (0, LANES - N_EXPERTS)))),
              br=jnp.pad(b_router.astype(F32), (0, LANES - N_EXPERTS))[None, :],
              wg=w_gate_e.astype(BF16), wu=w_up_e.astype(BF16), wd=w_down_e.astype(BF16),
              lng=mc["lng"], lnb=mc["lnb"])

    n_pool = cache_k.shape[1]
    cache_k4 = cache_k.reshape(depth, n_pool, PAGE_SIZE * DA_HEADS, LANES)
    cache_v4 = cache_v.reshape(depth, n_pool, PAGE_SIZE * DA_HEADS, LANES)
    pt_flat = page_table.reshape(-1).astype(jnp.int32)

    xp = x_prompt.reshape(mp, D_MODEL)
    xs = x_sample.reshape(ms, D_MODEL)
    zero_h = jnp.zeros((bp * SSM_GROUPS, LANES), F32)
    zero_buf = jnp.zeros((bp, 8, DN_QKV), F32)
    zero_s = jnp.zeros((bp, DN_HEADS, DN_DK, DN_DK), F32)
    outs = {k: [] for k in ("p_re", "p_im", "p_conv", "p_delta",
                            "s_re", "s_im", "s_conv", "s_delta", "s_k", "s_v")}
    pkv = None
    for l in range(depth):
        u, ab = _project(xp, w_main, w_ab, l, min(1024, mp))
        ya, h_re, h_im = _ssm_prompt(u, sc, l, bp, t, zero_h)
        yb, s_fin = _dn_prompt(u, ab, dc, l, bp, t, zero_buf, zero_s)
        yc = _da_prompt(u, ac, l, bp, t, lam_inits[l])
        xm = _merge(xp, ya, yb, yc, mc, l, 256)
        xp = _moe(xm, ec, l, 512)
        u3 = u.reshape(bp, t, N_MAIN)
        outs["p_re"].append(h_re)
        outs["p_im"].append(h_im)
        outs["p_conv"].append(u3[:, t - (CONV_W - 1):, :DN_QKV])
        outs["p_delta"].append(s_fin)
        pkv = _kv_out(u, l, depth, pkv)
        us, abs_ = _project(xs, w_main, w_ab, l, ms)
        ya_s, n_re, n_im = _ssm_step(us, sc, l, state_ssm_re[l].reshape(ms, -1).astype(F32),
                                     state_ssm_im[l].reshape(ms, -1).astype(F32))
        bufs = [state_conv[l, :, j, :].astype(F32) for j in range(CONV_W - 1)]
        yb_s, s_new = _dn_step(us, abs_, dc, l, bufs, state_delta)
        q3 = us[:, None, COL_DAQ * 512:(COL_DAQ + 1) * 512]
        k3 = us[:, None, COL_DAK * 512:(COL_DAK + 1) * 512]
        v3 = us[:, None, COL_DAV * 512:(COL_DAV + 1) * 512]
        yc_s = _da_step(q3, k3, v3, cache_k4, cache_v4, pt_flat, ac, l, lam_inits[l]).reshape(ms, 512)
        xm_s = _merge(xs, ya_s, yb_s, yc_s, mc, l, ms)
        xs = _moe(xm_s, ec, l, ms)
        outs["s_re"].append(n_re.reshape(ms, SSM_GROUPS, SSM_STATE))
        outs["s_im"].append(n_im.reshape(ms, SSM_GROUPS, SSM_STATE))
        outs["s_conv"].append(jnp.concatenate([state_conv[l, :, 1:, :].astype(F32), us[:, None, :DN_QKV]], axis=1))
        outs["s_delta"].append(s_new)
        outs["s_k"].append(k3.reshape(ms, 1, DA_HEADS, LANES))
        outs["s_v"].append(v3.reshape(ms, 1, DA_HEADS, LANES))

    st = lambda k, dt: jnp.stack(outs[k], axis=0).astype(dt)
    return (xp.reshape(bp, t, D_MODEL), xs.reshape(ms, 1, D_MODEL),
            st("p_re", state_ssm_re.dtype), st("p_im", state_ssm_im.dtype), st("p_conv", state_conv.dtype),
            st("p_delta", state_delta.dtype),
            pkv[0].reshape(depth, bp, t, DA_HEADS, LANES).astype(cache_k.dtype),
            pkv[1].reshape(depth, bp, t, DA_HEADS, LANES).astype(cache_v.dtype),
            st("s_re", state_ssm_re.dtype), st("s_im", state_ssm_im.dtype), st("s_conv", state_conv.dtype),
            st("s_delta", state_delta.dtype), st("s_k", cache_k.dtype), st("s_v", cache_v.dtype))
```

```python
import functools
import math

import jax
import jax.numpy as jnp
from jax import lax
from jax.experimental import pallas as pl
from jax.experimental.pallas import tpu as pltpu

F32 = jnp.float32
BF16 = jnp.bfloat16
SDS = jax.ShapeDtypeStruct

D_MODEL = 1024
DEPTH = 4
MIX_WIDTH = 512
N_BRANCH = 3
SSM_GROUP_CH = 16
SSM_GROUPS = 32
SSM_STATE = 64
SSM_CHUNK = 16
DN_DK = 128
DN_HEADS = 4
DN_QKV = 1536
CONV_W = 4
DN_CHUNK = 64
DN_BLOCK = 256
DA_DH = 64
DA_HEADS = 4
N_EXPERTS = 16
EXPERTS_PER_GROUP = 4
D_EXPERT = 512
PAGE_SIZE = 128
DEEPNORM_ALPHA = (2 * DEPTH) ** 0.25
LN_EPS = 1e-5
RMS_EPS = 1e-6
LANES = 128
MIB = 1024 * 1024

COL_DN = 0
COL_DAQ, COL_DAK, COL_DAV = 3, 4, 5
COL_SSM = 6
COL_Z = 7
N_MAIN = 8 * 512


def _cparams(sem, vmem_mib=48):
    return pltpu.CompilerParams(dimension_semantics=sem, vmem_limit_bytes=vmem_mib * MIB)


NN = (((1,), (0,)), ((), ()))
NT = (((1,), (1,)), ((), ()))
TN = (((0,), (0,)), ((), ()))


def _mm1(a, b, dims=NN):
    return lax.dot_general(a.astype(BF16), b.astype(BF16), dims, preferred_element_type=F32)


def _split(a):
    hi = a.astype(BF16)
    return hi, (a - hi.astype(F32)).astype(BF16)


def _mm3(a, b, dims=NN):
    ah, al = a if isinstance(a, tuple) else _split(a)
    bh, bl = b if isinstance(b, tuple) else _split(b)
    d = functools.partial(lax.dot_general, dimension_numbers=dims, preferred_element_type=F32)
    return d(ah, bh) + (d(al, bh) + d(ah, bl))


def _split_body(w_ref, o_ref):
    hi, lo = _split(w_ref[...])
    o_ref[0] = hi
    o_ref[1] = lo


def _split_hbm(w):
    shape = w.shape
    w2 = w.astype(F32).reshape(-1, shape[-1])
    rows, cols = w2.shape
    br = min(rows, 512)
    out = pl.pallas_call(
        _split_body,
        out_shape=SDS((2, rows, cols), BF16),
        grid=(rows // br,),
        in_specs=[pl.BlockSpec((br, cols), lambda i: (i, 0))],
        out_specs=pl.BlockSpec((2, br, cols), lambda i: (0, i, 0)),
        compiler_params=_cparams(("parallel",)),
        name="split",
    )(w2)
    return out.reshape((2,) + shape)


def _dot_hi(a, b):
    return jnp.dot(a, b, preferred_element_type=F32, precision=lax.Precision.HIGHEST)


_mm_ssm = _mm1
_mm_dn = _mm1
_mm_da = _mm1
_mm_moe = _mm1


def _layer_norm(x, g, b):
    mu = jnp.mean(x, axis=-1, keepdims=True)
    xc = x - mu
    var = jnp.mean(xc * xc, axis=-1, keepdims=True)
    return xc * lax.rsqrt(var + LN_EPS) * g + b


def _rms_norm(x, g):
    return x * lax.rsqrt(jnp.mean(x * x, axis=-1, keepdims=True) + RMS_EPS) * g


def _proj_body(x_ref, w_ref, wab_ref, u_ref, ab_ref, xh_ref, xl_ref):
    @pl.when(pl.program_id(1) == 0)
    def _():
        xh, xl = _split(x_ref[...])
        xh_ref[...] = xh
        xl_ref[...] = xl
        ab_ref[...] = _mm3((xh, xl), (wab_ref[0, 0], wab_ref[1, 0]))

    u_ref[...] = _mm3((xh_ref[...], xl_ref[...]), (w_ref[0, 0], w_ref[1, 0]))


def _project(x, w_main, w_ab, layer, tm):
    m = x.shape[0]
    tn = 512
    return pl.pallas_call(
        _proj_body,
        out_shape=(SDS((m, N_MAIN), F32), SDS((m, LANES), F32)),
        grid=(m // tm, N_MAIN // tn),
        in_specs=[pl.BlockSpec((tm, D_MODEL), lambda i, j: (i, 0)),
                  pl.BlockSpec((2, 1, D_MODEL, tn), lambda i, j: (0, layer, 0, j)),
                  pl.BlockSpec((2, 1, D_MODEL, LANES), lambda i, j: (0, layer, 0, 0))],
        out_specs=(pl.BlockSpec((tm, tn), lambda i, j: (i, j)),
                   pl.BlockSpec((tm, LANES), lambda i, j: (i, 0))),
        scratch_shapes=[pltpu.VMEM((tm, D_MODEL), BF16), pltpu.VMEM((tm, D_MODEL), BF16)],
        compiler_params=_cparams(("parallel", "arbitrary")),
        name="proj",
    )(x, w_main, w_ab)


def _kv_out_body(k_ref, v_ref, *refs):
    ko_ref, vo_ref = refs[-2], refs[-1]
    tm = k_ref.shape[0]
    for h in range(DA_HEADS):
        ko_ref[0, pl.ds(h, tm, stride=DA_HEADS), :] = k_ref[:, h * LANES:(h + 1) * LANES]
        vo_ref[0, pl.ds(h, tm, stride=DA_HEADS), :] = v_ref[:, h * LANES:(h + 1) * LANES]


def _kv_out(u, layer, depth, prev):
    m = u.shape[0]
    tm = min(512, m)
    shape = SDS((depth, m * DA_HEADS, LANES), F32)
    in_specs = [pl.BlockSpec((tm, 512), lambda i: (i, COL_DAK)), pl.BlockSpec((tm, 512), lambda i: (i, COL_DAV))]
    args = [u, u]
    aliases = {}
    if prev is not None:
        in_specs += [pl.BlockSpec(memory_space=pl.ANY)] * 2
        args += list(prev)
        aliases = {2: 0, 3: 1}
    out_spec = pl.BlockSpec((1, tm * DA_HEADS, LANES), lambda i: (layer, i, 0))
    return pl.pallas_call(
        _kv_out_body,
        out_shape=(shape, shape),
        grid=(m // tm,),
        in_specs=in_specs,
        out_specs=(out_spec, out_spec),
        input_output_aliases=aliases,
        compiler_params=_cparams(("parallel",)),
        name="kv_out",
    )(*args)


def _ssm_constants(a_re, a_im, log_dt, b_re, b_im, c_re, c_im, d):
    hp = lax.Precision.HIGHEST
    ell = SSM_CHUNK
    a_r, a_i = a_re.astype(F32), a_im.astype(F32)
    dt = jnp.exp(log_dt.astype(F32))[..., None]
    steps = jnp.arange(ell + 1, dtype=F32)
    mag = jnp.exp((dt * a_r)[..., None] * steps)
    ang = (dt * a_i)[..., None] * steps
    pw_r, pw_i = mag * jnp.cos(ang), mag * jnp.sin(ang)
    ab_r, ab_i = pw_r[..., 1], pw_i[..., 1]
    den = a_r * a_r + a_i * a_i
    z_r = ((ab_r - 1.0) * a_r + ab_i * a_i) / den
    z_i = (ab_i * a_r - (ab_r - 1.0) * a_i) / den
    bb_r, bb_i = b_re.astype(F32), b_im.astype(F32)
    bbar_r = z_r[..., None] * bb_r - z_i[..., None] * bb_i
    bbar_i = z_r[..., None] * bb_i + z_i[..., None] * bb_r
    c_r, c_i = c_re.astype(F32), c_im.astype(F32)
    dd, g, n, ch = bbar_r.shape
    pj_r, pj_i = pw_r[..., :ell, None], pw_i[..., :ell, None]
    pb_r = pj_r * bbar_r[:, :, :, None, :] - pj_i * bbar_i[:, :, :, None, :]
    pb_i = pj_r * bbar_i[:, :, :, None, :] + pj_i * bbar_r[:, :, :, None, :]
    kern = (jnp.einsum("dgcn,dgnjk->dgjck", c_r, pb_r, precision=hp)
            - jnp.einsum("dgcn,dgnjk->dgjck", c_i, pb_i, precision=hp))
    s_idx = jnp.arange(ell)[:, None, None]
    t_idx = jnp.arange(ell)[None, :, None]
    shift = (t_idx - s_idx == jnp.arange(ell)[None, None, :]).astype(F32)
    toep = jnp.einsum("stj,dgjck->dgsktc", shift, kern, precision=hp).reshape(dd, g, ell * ch, ell * ch)
    e_r = pb_r[:, :, :, ::-1, :].transpose(0, 1, 3, 4, 2)
    e_i = pb_i[:, :, :, ::-1, :].transpose(0, 1, 3, 4, 2)
    e_mat = jnp.concatenate([e_r, e_i], axis=-1).reshape(dd, g, ell * ch, 2 * n)
    pt_r, pt_i = pw_r[:, :, None, :, 1:ell + 1], pw_i[:, :, None, :, 1:ell + 1]
    cp_r = (c_r[..., None] * pt_r - c_i[..., None] * pt_i).transpose(0, 1, 3, 4, 2)
    cp_i = (c_r[..., None] * pt_i + c_i[..., None] * pt_r).transpose(0, 1, 3, 4, 2)
    f_mat = jnp.concatenate([cp_r, -cp_i], axis=2).reshape(dd, g, 2 * n, ell * ch)
    al_r, al_i = pw_r[..., ell], pw_i[..., ell]
    ar = jnp.concatenate([al_r, al_r], axis=-1)
    ai = jnp.concatenate([-al_i, al_i], axis=-1)
    d_t = jnp.tile(d.astype(F32).reshape(dd, g, 1, ch), (1, 1, ell, 1)).reshape(dd, g, 1, ell * ch)
    eye = jnp.eye(g, dtype=F32)
    bre = jnp.einsum("dgnk,gh->dgkhn", bbar_r, eye, precision=hp).reshape(dd, g * ch, g * n)
    bim = jnp.einsum("dgnk,gh->dgkhn", bbar_i, eye, precision=hp).reshape(dd, g * ch, g * n)
    cre = jnp.einsum("dgcn,gh->dgnhc", c_r, eye, precision=hp).reshape(dd, g * n, g * ch)
    cim = jnp.einsum("dgcn,gh->dgnhc", c_i, eye, precision=hp).reshape(dd, g * n, g * ch)
    return dict(toep=toep, e=e_mat, f=f_mat, ar=ar, ai=ai, d_t=d_t, bre=bre, bim=bim, cre=cre, cim=cim,
                a1r=ab_r.reshape(dd, 1, g * n), a1i=ab_i.reshape(dd, 1, g * n),
                d=d.astype(F32).reshape(dd, 1, g * ch))


def _ssm_state_body(u_ref, e_ref, s_ref):
    s_ref[...] = _mm_ssm(u_ref[0], e_ref[0, 0])


def _ssm_scan_body(s_ref, h0_ref, ar_ref, ai_ref, hp_ref, hf_ref):
    ar = ar_ref[...]
    ai = ai_ref[...]

    def body(c, h):
        hp_ref[c] = h
        return ar * h + ai * pltpu.roll(h, SSM_STATE, 1) + s_ref[c]

    hf_ref[...] = lax.fori_loop(0, s_ref.shape[0], body, h0_ref[...])


def _ssm_out_body(u_ref, t_ref, h_ref, f_ref, d_ref, y_ref):
    u = u_ref[0]
    y_ref[0] = _mm_ssm(u, t_ref[0, 0]) + _mm_ssm(h_ref[...], f_ref[0, 0]) + u * d_ref[0, 0]


def _ssm_prompt(u, sc, layer, bsz, t, h0):
    ell, g, ch = SSM_CHUNK, SSM_GROUPS, SSM_GROUP_CH
    nct = t // ell
    nc = bsz * nct
    w = ell * ch
    ug = u[:, COL_SSM * 512:(COL_SSM + 1) * 512].reshape(bsz, nct, ell, g, ch)
    ug = ug.transpose(3, 0, 1, 2, 4).reshape(g, nc, w)
    s = pl.pallas_call(
        _ssm_state_body,
        out_shape=SDS((nc, g * LANES), F32),
        grid=(g,),
        in_specs=[pl.BlockSpec((1, nc, w), lambda i: (i, 0, 0)),
                  pl.BlockSpec((1, 1, w, LANES), lambda i: (layer, i, 0, 0))],
        out_specs=pl.BlockSpec((nc, LANES), lambda i: (0, i)),
        compiler_params=_cparams(("parallel",)),
        name="ssm_state",
    )(ug, sc["e"])
    s_t = s.reshape(bsz, nct, g, LANES).transpose(1, 0, 2, 3).reshape(nct, bsz * g, LANES)
    rows = bsz * g
    rb = min(64, rows)
    ar = jnp.tile(sc["ar"][layer], (bsz, 1))
    ai = jnp.tile(sc["ai"][layer], (bsz, 1))
    hp, hf = pl.pallas_call(
        _ssm_scan_body,
        out_shape=(SDS((nct, rows, LANES), F32), SDS((rows, LANES), F32)),
        grid=(rows // rb,),
        in_specs=[pl.BlockSpec((nct, rb, LANES), lambda i: (0, i, 0)),
                  pl.BlockSpec((rb, LANES), lambda i: (i, 0)),
                  pl.BlockSpec((rb, LANES), lambda i: (i, 0)),
                  pl.BlockSpec((rb, LANES), lambda i: (i, 0))],
        out_specs=(pl.BlockSpec((nct, rb, LANES), lambda i: (0, i, 0)),
                   pl.BlockSpec((rb, LANES), lambda i: (i, 0))),
        compiler_params=_cparams(("parallel",)),
        name="ssm_scan",
    )(s_t, h0, ar, ai)
    hprev = hp.reshape(nct, bsz, g, LANES).transpose(1, 0, 2, 3).reshape(nc, g * LANES)
    y = pl.pallas_call(
        _ssm_out_body,
        out_shape=SDS((g, nc, w), F32),
        grid=(g,),
        in_specs=[pl.BlockSpec((1, nc, w), lambda i: (i, 0, 0)),
                  pl.BlockSpec((1, 1, w, w), lambda i: (layer, i, 0, 0)),
                  pl.BlockSpec((nc, LANES), lambda i: (0, i)),
                  pl.BlockSpec((1, 1, LANES, w), lambda i: (layer, i, 0, 0)),
                  pl.BlockSpec((1, 1, 1, w), lambda i: (layer, i, 0, 0))],
        out_specs=pl.BlockSpec((1, nc, w), lambda i: (i, 0, 0)),
        compiler_params=_cparams(("parallel",)),
        name="ssm_out",
    )(ug, sc["toep"], hprev, sc["f"], sc["d_t"])
    y = y.reshape(g, bsz, nct, ell, ch).transpose(1, 2, 3, 0, 4).reshape(bsz * t, g * ch)
    hf = hf.reshape(bsz, g, LANES)
    return y, hf[..., :SSM_STATE], hf[..., SSM_STATE:]


def _ssm_step_body(u_ref, hr_ref, hi_ref, ar_ref, ai_ref, bre_ref, bim_ref, cre_ref, cim_ref, d_ref,
                   y_ref, nr_ref, ni_ref):
    u = u_ref[...]
    hr, hi = hr_ref[...], hi_ref[...]
    ar, ai = ar_ref[0], ai_ref[0]
    nr = ar * hr - ai * hi + _mm_ssm(u, bre_ref[0])
    ni = ar * hi + ai * hr + _mm_ssm(u, bim_ref[0])
    nr_ref[...] = nr
    ni_ref[...] = ni
    y_ref[...] = _mm_ssm(nr, cre_ref[0]) - _mm_ssm(ni, cim_ref[0]) + u * d_ref[0]


def _ssm_step(u, sc, layer, h_re, h_im):
    ms = u.shape[0]
    gn = SSM_GROUPS * SSM_STATE
    full = lambda shape: pl.BlockSpec(shape, lambda i: (0,) * len(shape))
    lay = lambda shape: pl.BlockSpec((1,) + shape, lambda i: (layer,) + (0,) * len(shape))
    return pl.pallas_call(
        _ssm_step_body,
        out_shape=(SDS((ms, 512), F32), SDS((ms, gn), F32), SDS((ms, gn), F32)),
        grid=(1,),
        in_specs=[pl.BlockSpec((ms, 512), lambda i: (0, COL_SSM)), full((ms, gn)), full((ms, gn)),
                  lay((1, gn)), lay((1, gn)), lay((512, gn)), lay((512, gn)), lay((gn, 512)), lay((gn, 512)),
                  lay((1, 512))],
        out_specs=(full((ms, 512)), full((ms, gn)), full((ms, gn))),
        compiler_params=_cparams(("arbitrary",)),
        name="ssm_step",
    )(u, h_re, h_im, sc["a1r"], sc["a1i"], sc["bre"], sc["bim"], sc["cre"], sc["cim"], sc["d"])


def _shift_rows(x, prev8, s):
    if s == 0:
        return x
    xs = pltpu.roll(x, s, 0)
    ps = pltpu.roll(prev8, s, 0)
    row = lax.broadcasted_iota(jnp.int32, ps.shape, 0)
    head = jnp.where(row < s, ps, xs[:8])
    return jnp.concatenate([head, xs[8:]], axis=0)


def _dn_prompt_body(x_ref, prev_ref, buf_ref, ab_ref, abt_ref, z_ref, s0_ref, cw_ref, al_ref, dtb_ref,
                    alc_ref, dtbc_ref, ng_ref, y_ref, sf_ref, s_scr):
    i = pl.program_id(1)
    nb = pl.num_programs(1)
    tb = DN_BLOCK
    nck = tb // DN_CHUNK

    @pl.when(i == 0)
    def _():
        s_scr[...] = s0_ref[0]

    x = x_ref[...]
    prev8 = jnp.where(i == 0, buf_ref[0], prev_ref[...])
    cw = cw_ref[0]
    conv = x * cw[3:4]
    for j in range(CONV_W - 1):
        conv = conv + _shift_rows(x, prev8, CONV_W - 1 - j) * cw[j:j + 1]
    y = conv * jax.nn.sigmoid(conv)

    ab = ab_ref[...]
    gmat = -jnp.exp(al_ref[0]) * jax.nn.softplus(ab + dtb_ref[0])
    beta_m = jax.nn.sigmoid(ab)
    abt = abt_ref[0]
    g_rows = -jnp.exp(alc_ref[0]) * jax.nn.softplus(abt + dtbc_ref[0])

    r = lax.broadcasted_iota(jnp.int32, (tb, tb), 0)
    c = lax.broadcasted_iota(jnp.int32, (tb, tb), 1)
    same = (r // DN_CHUNK) == (c // DN_CHUNK)
    incl = same & (r >= c)
    strict = same & (r > c)
    l_incl = jnp.where(incl, 1.0, 0.0).astype(F32)
    u_incl = jnp.where(same & (c >= r), 1.0, 0.0).astype(F32)
    blk1 = jnp.where(same, 1.0, 0.0).astype(F32)
    cum_c = _dot_hi(l_incl, gmat)
    tot_c = _dot_hi(blk1, gmat)
    cum_r = _dot_hi(g_rows, u_incl)
    rowi = lax.broadcasted_iota(jnp.int32, (tb, 1), 0)

    heads = range(DN_HEADS)
    q, k, v, beta, gam, decay, tot, cc = [], [], [], [], [], [], [], []
    for h in heads:
        qh = y[:, h * DN_DK:(h + 1) * DN_DK]
        kh = y[:, 512 + h * DN_DK:512 + (h + 1) * DN_DK]
        q.append(qh * lax.rsqrt(jnp.sum(qh * qh, axis=-1, keepdims=True) + RMS_EPS) * (DN_DK ** -0.5))
        k.append(kh * lax.rsqrt(jnp.sum(kh * kh, axis=-1, keepdims=True) + RMS_EPS))
        v.append(y[:, 1024 + h * DN_DK:1024 + (h + 1) * DN_DK])
        cc.append(cum_c[:, h:h + 1])
        tot.append(tot_c[:, h:h + 1])
        beta.append(beta_m[:, 4 + h:5 + h])
        gam.append(jnp.exp(cc[h]))
        decay.append(jnp.where(incl, jnp.exp(jnp.where(incl, cc[h] - cum_r[h:h + 1, :], 0.0)), 0.0))
    kk = [_mm_dn(k[h], k[h], NT) for h in heads]
    qk = [_mm_dn(q[h], k[h], NT) for h in heads]
    p_mat = [jnp.where(strict, beta[h] * decay[h] * kk[h], 0.0) for h in heads]
    n_mat = [-p_mat[h] for h in heads]
    for _ in range(5):
        p_mat = [_mm_dn(p_mat[h], p_mat[h]) for h in heads]
        n_mat = [n_mat[h] + p_mat[h] + _mm_dn(n_mat[h], p_mat[h]) for h in heads]
    rhs = [jnp.concatenate([beta[h] * v[h], (beta[h] * gam[h]) * k[h]], axis=1) for h in heads]
    sol = [rhs[h] + _mm_dn(n_mat[h], rhs[h]) for h in heads]
    qk = [jnp.where(incl, qk[h] * decay[h], 0.0) for h in heads]
    qg = [q[h] * gam[h] for h in heads]
    k_dec = [k[h] * jnp.exp(tot[h] - cc[h]) for h in heads]
    e_tot = [jnp.exp(tot[h]) for h in heads]
    s = [s_scr[h] for h in heads]
    o_inter = [[] for _ in heads]
    ws = [[] for _ in heads]
    for ck in range(nck):
        lo, hi = ck * DN_CHUNK, (ck + 1) * DN_CHUNK
        w_c = [sol[h][lo:hi, :DN_DK] - _mm_dn(sol[h][lo:hi, DN_DK:], s[h]) for h in heads]
        for h in heads:
            o_inter[h].append(_mm_dn(qg[h][lo:hi], s[h]))
            ws[h].append(w_c[h])
        s = [e_tot[h][lo:lo + 1] * s[h] + _mm_dn(k_dec[h][lo:hi], w_c[h], TN) for h in heads]
    outs = []
    for h in heads:
        s_scr[h] = s[h]
        o = jnp.concatenate(o_inter[h], axis=0) + _mm_dn(qk[h], jnp.concatenate(ws[h], axis=0))
        zh = z_ref[:, h * DN_DK:(h + 1) * DN_DK]
        outs.append(_rms_norm(o, ng_ref[0]) * (zh * jax.nn.sigmoid(zh)))
    y_ref[...] = jnp.concatenate(outs, axis=1)

    @pl.when(i == nb - 1)
    def _():
        sf_ref[0] = s_scr[...]


def _dn_prompt(u, ab, dc, layer, bsz, t, conv_buf8, s0):
    tb = DN_BLOCK
    nb = t // tb
    abt = ab[:, :8].reshape(bsz, t, 8).transpose(0, 2, 1)
    lay = lambda shape: pl.BlockSpec((1,) + shape, lambda b, i: (layer,) + (0,) * len(shape))
    return pl.pallas_call(
        _dn_prompt_body,
        out_shape=(SDS((bsz * t, 512), F32), SDS((bsz, DN_HEADS, DN_DK, DN_DK), F32)),
        grid=(bsz, nb),
        in_specs=[pl.BlockSpec((tb, DN_QKV), lambda b, i: (b * nb + i, 0)),
                  pl.BlockSpec((8, DN_QKV), lambda b, i: (jnp.maximum((b * nb + i) * (tb // 8) - 1, 0), 0)),
                  pl.BlockSpec((1, 8, DN_QKV), lambda b, i: (b, 0, 0)),
                  pl.BlockSpec((tb, LANES), lambda b, i: (b * nb + i, 0)),
                  pl.BlockSpec((1, 8, tb), lambda b, i: (b, 0, i)),
                  pl.BlockSpec((tb, 512), lambda b, i: (b * nb + i, COL_Z)),
                  pl.BlockSpec((1, DN_HEADS, DN_DK, DN_DK), lambda b, i: (b, 0, 0, 0)),
                  lay((8, DN_QKV)), lay((1, LANES)), lay((1, LANES)), lay((8, tb)), lay((8, tb)),
                  lay((1, LANES))],
        out_specs=(pl.BlockSpec((tb, 512), lambda b, i: (b * nb + i, 0)),
                   pl.BlockSpec((1, DN_HEADS, DN_DK, DN_DK), lambda b, i: (b, 0, 0, 0))),
        scratch_shapes=[pltpu.VMEM((DN_HEADS, DN_DK, DN_DK), F32)],
        compiler_params=_cparams(("parallel", "arbitrary")),
        name="dn_prompt",
    )(u, u, conv_buf8, ab, abt, u, s0, dc["cw"], dc["al"], dc["dtb"], dc["alc"], dc["dtbc"], dc["ng"])


def _dn_step_body(x_ref, b0_ref, b1_ref, b2_ref, ab_ref, z_ref, s0_ref, cw_ref, al_ref, dtb_ref, ng_ref,
                  y_ref, sn_ref):
    cw = cw_ref[0]
    conv = b0_ref[...] * cw[0:1] + b1_ref[...] * cw[1:2] + b2_ref[...] * cw[2:3] + x_ref[...] * cw[3:4]
    y = conv * jax.nn.sigmoid(conv)
    ab = ab_ref[...]
    gam_m = jnp.exp(-jnp.exp(al_ref[0]) * jax.nn.softplus(ab + dtb_ref[0]))
    beta_m = jax.nn.sigmoid(ab)
    nrow = x_ref.shape[0]
    row = lax.broadcasted_iota(jnp.int32, (nrow, DN_DK), 0)
    outs = []
    for h in range(DN_HEADS):
        qh = y[:, h * DN_DK:(h + 1) * DN_DK]
        kh = y[:, 512 + h * DN_DK:512 + (h + 1) * DN_DK]
        vh = y[:, 1024 + h * DN_DK:1024 + (h + 1) * DN_DK]
        qh = qh * lax.rsqrt(jnp.sum(qh * qh, axis=-1, keepdims=True) + RMS_EPS) * (DN_DK ** -0.5)
        kh = kh * lax.rsqrt(jnp.sum(kh * kh, axis=-1, keepdims=True) + RMS_EPS)
        gam = gam_m[:, h:h + 1]
        beta = beta_m[:, 4 + h:5 + h]
        ks = jnp.zeros((nrow, DN_DK), F32)
        qs = jnp.zeros((nrow, DN_DK), F32)
        for n in range(nrow):
            s_n = s0_ref[0, n, h]
            ks = jnp.where(row == n, _mm_dn(kh, s_n), ks)
            qs = jnp.where(row == n, _mm_dn(qh, s_n), qs)
        w = beta * vh - (beta * gam) * ks
        o = gam * qs + jnp.sum(qh * kh, axis=-1, keepdims=True) * w
        for n in range(nrow):
            k_only = jnp.where(row == n, kh, 0.0)
            sn_ref[n, h] = gam[n:n + 1] * s0_ref[0, n, h] + _mm_dn(k_only, w, TN)
        zh = z_ref[:, h * DN_DK:(h + 1) * DN_DK]
        outs.append(_rms_norm(o, ng_ref[0]) * (zh * jax.nn.sigmoid(zh)))
    y_ref[...] = jnp.concatenate(outs, axis=1)


def _dn_step(u, ab, dc, layer, bufs, s0_all):
    ms = u.shape[0]
    nr = 8
    lay = lambda shape: pl.BlockSpec((1,) + shape, lambda i: (layer,) + (0,) * len(shape))
    rows = lambda w: pl.BlockSpec((nr, w), lambda i: (i, 0))
    return pl.pallas_call(
        _dn_step_body,
        out_shape=(SDS((ms, 512), F32), SDS((ms, DN_HEADS, DN_DK, DN_DK), F32)),
        grid=(ms // nr,),
        in_specs=[rows(DN_QKV), rows(DN_QKV), rows(DN_QKV), rows(DN_QKV), rows(LANES),
                  pl.BlockSpec((nr, 512), lambda i: (i, COL_Z)),
                  pl.BlockSpec((1, nr, DN_HEADS, DN_DK, DN_DK), lambda i: (layer, i, 0, 0, 0)),
                  lay((8, DN_QKV)), lay((1, LANES)), lay((1, LANES)), lay((1, LANES))],
        out_specs=(rows(512), pl.BlockSpec((nr, DN_HEADS, DN_DK, DN_DK), lambda i: (i, 0, 0, 0))),
        compiler_params=_cparams(("parallel",)),
        name="dn_step",
    )(u, bufs[0], bufs[1], bufs[2], ab, u, s0_all, dc["cw"], dc["al"], dc["dtb"], dc["ng"])


def _da_prompt_body(q_ref, k_ref, v_ref, lam_ref, ng_ref, o_ref, qm_ref, m_ref, l_ref, acc_ref, *, lam_init):
    qi = pl.program_id(1)
    ki = pl.program_id(2)
    tq = q_ref.shape[0]
    tk = k_ref.shape[0]

    @pl.when(ki == 0)
    def _():
        lane = lax.broadcasted_iota(jnp.int32, (tq, LANES), 1)
        for h in range(DA_HEADS):
            qh = q_ref[:, h * LANES:(h + 1) * LANES] * (DA_DH ** -0.5)
            qm_ref[2 * h] = jnp.where(lane < DA_DH, qh, 0.0).astype(BF16)
            qm_ref[2 * h + 1] = jnp.where(lane >= DA_DH, qh, 0.0).astype(BF16)
        m_ref[...] = jnp.full(m_ref.shape, -jnp.inf, F32)
        l_ref[...] = jnp.zeros(l_ref.shape, F32)
        acc_ref[...] = jnp.zeros(acc_ref.shape, F32)

    def step(diagonal):
        kb = k_ref[...].astype(BF16)
        vb = v_ref[...].astype(BF16)
        if diagonal:
            r = lax.broadcasted_iota(jnp.int32, (tk, tq), 0)
            c = lax.broadcasted_iota(jnp.int32, (tk, tq), 1)
            keep = r <= c
        for h in range(DA_HEADS):
            kh = kb[:, h * LANES:(h + 1) * LANES]
            vh = vb[:, h * LANES:(h + 1) * LANES]
            for mp in range(2):
                idx = 2 * h + mp
                s = _mm_da(kh, qm_ref[idx], NT)
                if diagonal:
                    s = jnp.where(keep, s, -jnp.inf)
                m_prev = m_ref[idx:idx + 1, :]
                m_new = jnp.maximum(m_prev, jnp.max(s, axis=0, keepdims=True))
                alpha = jnp.exp(m_prev - m_new)
                p = jnp.exp(s - m_new)
                l_ref[idx:idx + 1, :] = alpha * l_ref[idx:idx + 1, :] + jnp.sum(p, axis=0, keepdims=True)
                acc_ref[idx] = alpha * acc_ref[idx] + _mm_da(vh, p, TN)
                m_ref[idx:idx + 1, :] = m_new

    @pl.when(ki < qi)
    def _():
        step(False)

    @pl.when(ki == qi)
    def _():
        step(True)
        lam = lam_ref[0][:, 0:1]
        for h in range(DA_HEADS):
            o_t = (acc_ref[2 * h] / l_ref[2 * h:2 * h + 1, :]
                   - lam * (acc_ref[2 * h + 1] / l_ref[2 * h + 1:2 * h + 2, :]))
            o_ref[:, h * LANES:(h + 1) * LANES] = _rms_norm(o_t.T, ng_ref[0]) * (1.0 - lam_init)


def _da_prompt(u, ac, layer, bsz, t, lam_init):
    tq = 512
    nq = t // tq
    lay = lambda shape: pl.BlockSpec((1,) + shape, lambda b, i, j: (layer,) + (0,) * len(shape))
    return pl.pallas_call(
        functools.partial(_da_prompt_body, lam_init=lam_init),
        out_shape=SDS((bsz * t, 512), F32),
        grid=(bsz, nq, nq),
        in_specs=[pl.BlockSpec((tq, 512), lambda b, i, j: (b * nq + i, COL_DAQ)),
                  pl.BlockSpec((tq, 512), lambda b, i, j: (b * nq + jnp.minimum(i, j), COL_DAK)),
                  pl.BlockSpec((tq, 512), lambda b, i, j: (b * nq + jnp.minimum(i, j), COL_DAV)),
                  lay((1, LANES)), lay((1, LANES))],
        out_specs=pl.BlockSpec((tq, 512), lambda b, i, j: (b * nq + i, 0)),
        scratch_shapes=[pltpu.VMEM((2 * DA_HEADS, tq, LANES), BF16),
                        pltpu.VMEM((2 * DA_HEADS, tq), F32),
                        pltpu.VMEM((2 * DA_HEADS, tq), F32),
                        pltpu.VMEM((2 * DA_HEADS, LANES, tq), F32)],
        compiler_params=_cparams(("parallel", "parallel", "arbitrary")),
        name="da_prompt",
    )(u, u, u, ac["lam"], ac["ng"])


def _da_step_body(pt_ref, q_ref, kn_ref, vn_ref, lam_ref, ng_ref, *refs, n_pages, lam_init):
    k_pages = refs[:n_pages]
    v_pages = refs[n_pages:2 * n_pages]
    o_ref = refs[2 * n_pages]
    s_scr = refs[2 * n_pages + 1]
    nh = DA_HEADS
    ps = PAGE_SIZE
    row = lax.broadcasted_iota(jnp.int32, (LANES, LANES), 0)
    lane = lax.broadcasted_iota(jnp.int32, (LANES, LANES), 1)
    q = q_ref[0] * (DA_DH ** -0.5)
    qmats = []
    for h in range(nh):
        qh = jnp.broadcast_to(q[:, h * LANES:(h + 1) * LANES], (LANES, LANES))
        pick = ((row == 2 * h) & (lane < DA_DH)) | ((row == 2 * h + 1) & (lane >= DA_DH))
        qmats.append(jnp.where(pick, qh, 0.0).astype(BF16))
    for j in range(n_pages):
        acc = jnp.zeros((ps, LANES), F32)
        for h in range(nh):
            kh = k_pages[j][pl.ds(h, ps, stride=nh), :].astype(BF16)
            acc = acc + _mm1(kh, qmats[h], NT)
        s_scr[j * ps:(j + 1) * ps, :] = acc
    accn = jnp.zeros((8, LANES), F32)
    kn = kn_ref[0]
    for h in range(nh):
        kh = jnp.broadcast_to(kn[:, h * LANES:(h + 1) * LANES], (8, LANES)).astype(BF16)
        accn = accn + _mm1(kh, qmats[h], NT)
    row8 = lax.broadcasted_iota(jnp.int32, (8, LANES), 0)
    s_scr[n_pages * ps:n_pages * ps + 8, :] = jnp.where(row8 == 0, accn, -jnp.inf)

    s = s_scr[...]
    m = jnp.max(s, axis=0, keepdims=True)
    p = jnp.exp(s - m)
    pn = (p / jnp.sum(p, axis=0, keepdims=True)).astype(BF16)
    er = lax.broadcasted_iota(jnp.int32, (LANES, nh * LANES), 0)
    ec = lax.broadcasted_iota(jnp.int32, (LANES, nh * LANES), 1) // LANES
    x1 = jnp.where(er == 2 * ec, 1.0, 0.0).astype(BF16)
    x2 = jnp.where(er == 2 * ec + 1, 1.0, 0.0).astype(BF16)
    lam = lam_ref[0][:, 0:1]
    accs = [jnp.zeros((ps, LANES), F32) for _ in range(nh)]
    for j in range(n_pages):
        pj = pn[j * ps:(j + 1) * ps]
        cj = _mm1(pj, x1) - lam * _mm1(pj, x2)
        for h in range(nh):
            vh = v_pages[j][pl.ds(h, ps, stride=nh), :]
            accs[h] = accs[h] + cj[:, h * LANES:(h + 1) * LANES] * vh
    pj = pn[n_pages * ps:n_pages * ps + 8]
    cn = _mm1(pj, x1) - lam * _mm1(pj, x2)
    vn = vn_ref[0]
    for h in range(nh):
        o = jnp.sum(accs[h], axis=0, keepdims=True)
        o = o + jnp.sum(cn[:, h * LANES:(h + 1) * LANES] * vn[:, h * LANES:(h + 1) * LANES], axis=0, keepdims=True)
        o_ref[0, :, h * LANES:(h + 1) * LANES] = _rms_norm(o, ng_ref[0]) * (1.0 - lam_init)


def _da_step(q3, k3, v3, cache_k4, cache_v4, pt_flat, ac, layer, lam_init):
    ms = q3.shape[0]
    n_pages = pt_flat.shape[0] // ms
    rows = PAGE_SIZE * DA_HEADS
    tok = pl.BlockSpec((1, 1, 512), lambda b, pt: (b, 0, 0))
    lay = pl.BlockSpec((1, 1, LANES), lambda b, pt: (layer, 0, 0))

    def page_spec(j):
        return pl.BlockSpec((None, None, rows, LANES), lambda b, pt: (layer, pt[b * n_pages + j], 0, 0))

    grid_spec = pltpu.PrefetchScalarGridSpec(
        num_scalar_prefetch=1,
        grid=(ms,),
        in_specs=[tok, tok, tok, lay, lay] + [page_spec(j) for j in range(n_pages)]
        + [page_spec(j) for j in range(n_pages)],
        out_specs=tok,
        scratch_shapes=[pltpu.VMEM((n_pages * PAGE_SIZE + 8, LANES), F32)],
    )
    return pl.pallas_call(
        functools.partial(_da_step_body, n_pages=n_pages, lam_init=lam_init),
        out_shape=SDS((ms, 1, 512), F32),
        grid_spec=grid_spec,
        compiler_params=_cparams(("arbitrary",), 56),
        name="da_step",
    )(pt_flat, q3, k3, v3, ac["lam"], ac["ng"], *([cache_k4] * n_pages), *([cache_v4] * n_pages))


def _merge_body(x_ref, ya_ref, yb_ref, yc_ref, wg_ref, bg_ref, wglu_ref, bglu_ref, wbr_ref, wout_ref,
                lng_ref, lnb_ref, o_ref):
    x = x_ref[...]
    xs = _split(x)
    ya = jax.nn.gelu(ya_ref[...])
    ya = ya * jax.nn.sigmoid(_mm3(ya, (wglu_ref[0, 0], wglu_ref[1, 0])) + bglu_ref[0])
    ys = (ya, yb_ref[...], yc_ref[...])
    merged = jnp.zeros(x.shape, F32)
    for i in range(N_BRANCH):
        lo, hi = i * D_MODEL, (i + 1) * D_MODEL
        gate = jax.nn.sigmoid(_mm3(xs, (wg_ref[0, 0, :, lo:hi], wg_ref[1, 0, :, lo:hi])) + bg_ref[0, :, lo:hi])
        merged = merged + gate * _mm3(ys[i], (wbr_ref[0, 0, i], wbr_ref[1, 0, i]))
    mix = _mm3(merged, (wout_ref[0, 0], wout_ref[1, 0]))
    o_ref[...] = _layer_norm(DEEPNORM_ALPHA * x + mix, lng_ref[0, 0:1], lnb_ref[0, 0:1])


def _merge(x, ya, yb, yc, mc, layer, tm):
    m = x.shape[0]
    lay = lambda shape: pl.BlockSpec((1,) + shape, lambda i: (layer,) + (0,) * len(shape))
    wlay = lambda shape: pl.BlockSpec((2, 1) + shape, lambda i: (0, layer) + (0,) * len(shape),
                                      pipeline_mode=pl.Buffered(1))
    tok = lambda w: pl.BlockSpec((tm, w), lambda i: (i, 0))
    return pl.pallas_call(
        _merge_body,
        out_shape=SDS((m, D_MODEL), F32),
        grid=(m // tm,),
        in_specs=[tok(D_MODEL), tok(512), tok(512), tok(512),
                  wlay((D_MODEL, N_BRANCH * D_MODEL)), lay((1, N_BRANCH * D_MODEL)), wlay((512, 512)), lay((1, 512)),
                  wlay((N_BRANCH, 512, D_MODEL)), wlay((D_MODEL, D_MODEL)), lay((2, D_MODEL)), lay((2, D_MODEL))],
        out_specs=tok(D_MODEL),
        compiler_params=_cparams(("parallel",), 56),
        name="merge",
    )(x, ya, yb, yc, mc["wg"], mc["bg"], mc["wglu"], mc["bglu"], mc["wbr"], mc["wout"], mc["lng"], mc["lnb"])


def _router_gate(logits, b_router):
    lane = lax.broadcasted_iota(jnp.int32, logits.shape, 1)
    valid = lane < N_EXPERTS
    lg = jnp.where(valid, logits, -jnp.inf)
    ex = jnp.exp(lg - jnp.max(lg, axis=-1, keepdims=True))
    scores = ex / jnp.sum(ex, axis=-1, keepdims=True)
    neg = -1e30
    sel = jnp.where(valid, scores + b_router, neg)
    pos = lane % EXPERTS_PER_GROUP
    rank = jnp.zeros(logits.shape, F32)
    for s in range(1, EXPERTS_PER_GROUP):
        lo = pltpu.roll(sel, s, 1)
        hi = pltpu.roll(sel, LANES - s, 1)
        rank = rank + jnp.where((pos >= s) & (lo >= sel), 1.0, 0.0)
        rank = rank + jnp.where((pos + s < EXPERTS_PER_GROUP) & (hi > sel), 1.0, 0.0)
    top2 = rank < 2.0
    msel = jnp.where(top2, sel, 0.0)
    gs = msel
    for s in range(1, EXPERTS_PER_GROUP):
        gs = gs + jnp.where(pos >= s, pltpu.roll(msel, s, 1), 0.0)
        gs = gs + jnp.where(pos + s < EXPERTS_PER_GROUP, pltpu.roll(msel, LANES - s, 1), 0.0)
    gs = jnp.where(valid, gs, neg)
    beaten = jnp.zeros(logits.shape, F32)
    for s in range(EXPERTS_PER_GROUP, N_EXPERTS, EXPERTS_PER_GROUP):
        lo = pltpu.roll(gs, s, 1)
        hi = pltpu.roll(gs, LANES - s, 1)
        beaten = beaten + jnp.where((lane >= s) & (lo >= gs), 1.0, 0.0)
        beaten = beaten + jnp.where((lane + s < N_EXPERTS) & (hi > gs), 1.0, 0.0)
    keep = top2 & (beaten < 0.5) & valid
    w = jnp.where(keep, scores, 0.0)
    return w / jnp.sum(w, axis=-1, keepdims=True), jnp.where(keep, 1.0, 0.0)


def _moe_body(x_ref, wr_ref, br_ref, wg_ref, wu_ref, wd_ref, lng_ref, lnb_ref, o_ref, xb_ref, gate_ref, acc_ref):
    e = pl.program_id(1)

    @pl.when(e == 0)
    def _():
        xb_ref[...] = x_ref[...].astype(BF16)
        gate_ref[...] = _router_gate(_mm3(x_ref[...], (wr_ref[0], wr_ref[1])), br_ref[...])[0]
        acc_ref[...] = jnp.zeros(acc_ref.shape, F32)

    xb = xb_ref[...]
    hg = _mm_moe(xb, wg_ref[0, 0])
    hu = _mm_moe(xb, wu_ref[0, 0])
    gate = gate_ref[...]
    lane = lax.broadcasted_iota(jnp.int32, gate.shape, 1)
    gcol = jnp.sum(jnp.where(lane == e, gate, 0.0), axis=-1, keepdims=True)
    hid = hg * jax.nn.sigmoid(hg) * hu * gcol
    acc_ref[...] += _mm_moe(hid, wd_ref[0, 0])

    @pl.when(e == N_EXPERTS - 1)
    def _():
        o_ref[...] = _layer_norm(DEEPNORM_ALPHA * x_ref[...] + acc_ref[...], lng_ref[0, 1:2], lnb_ref[0, 1:2])


def _moe(x, ec, layer, tm):
    m = x.shape[0]
    lay2 = lambda shape: pl.BlockSpec((1,) + shape, lambda i, e: (layer,) + (0,) * len(shape))
    exp = lambda shape: pl.BlockSpec((1, 1) + shape, lambda i, e: (layer, e) + (0,) * len(shape))
    return pl.pallas_call(
        _moe_body,
        out_shape=SDS((m, D_MODEL), F32),
        grid=(m // tm, N_EXPERTS),
        in_specs=[pl.BlockSpec((tm, D_MODEL), lambda i, e: (i, 0)),
                  pl.BlockSpec((2, D_MODEL, LANES), lambda i, e: (0, 0, 0)),
                  pl.BlockSpec((1, LANES), lambda i, e: (0, 0)),
                  exp((D_MODEL, D_EXPERT)), exp((D_MODEL, D_EXPERT)), exp((D_EXPERT, D_MODEL)),
                  lay2((2, D_MODEL)), lay2((2, D_MODEL))],
        out_specs=pl.BlockSpec((tm, D_MODEL), lambda i, e: (i, 0)),
        scratch_shapes=[pltpu.VMEM((tm, D_MODEL), BF16), pltpu.VMEM((tm, LANES), F32),
                        pltpu.VMEM((tm, D_MODEL), F32)],
        compiler_params=_cparams(("parallel", "arbitrary")),
        name="moe",
    )(x, ec["wr"], ec["br"], ec["wg"], ec["wu"], ec["wd"], ec["lng"], ec["lnb"])


MOE_CHUNK = 128


def _moe_grouped_body(x_ref, wr_ref, br_ref, tril_ref, triu_ref, wg_ref, wu_ref, wd_ref, lng_ref, lnb_ref, o_ref,
                      xb_ref, gate_ref, rcol_ref, rrow_ref, xg_ref, gg_ref, yc_ref, acc_ref, cnt_ref):
    g = pl.program_id(1)
    j = pl.program_id(2)
    tm = x_ref.shape[0]
    ch = MOE_CHUNK
    n_groups = N_EXPERTS // EXPERTS_PER_GROUP

    @pl.when((g == 0) & (j == 0))
    def _():
        x = x_ref[...]
        xb_ref[...] = x.astype(BF16)
        gate, keep = _router_gate(_mm3(x, (wr_ref[0], wr_ref[1])), br_ref[...])
        gate_ref[...] = gate
        er = lax.broadcasted_iota(jnp.int32, (LANES, LANES), 0)
        ec = lax.broadcasted_iota(jnp.int32, (LANES, LANES), 1)
        to_group = jnp.where((er // EXPERTS_PER_GROUP == ec) & (er < N_EXPERTS), 0.5, 0.0)
        member = _mm1(keep, to_group)
        rank_c = _mm1(tril_ref[...], member)
        rcol_ref[...] = jnp.where(member > 0.5, rank_c, 0.0)
        member_t = member.T
        rank_r = _mm1(member_t, triu_ref[...])
        rrow_ref[...] = jnp.where(member_t > 0.5, rank_r, 0.0)
        for gi in range(n_groups):
            cnt_ref[gi] = jnp.sum(member[:, gi:gi + 1]).astype(jnp.int32)
        acc_ref[...] = jnp.zeros(acc_ref.shape, F32)

    n_chunks = (cnt_ref[g] + (ch - 1)) // ch
    lane = lax.broadcasted_iota(jnp.int32, (ch, LANES), 1)

    @pl.when(j == 0)
    def _():
        rrow = rrow_ref[pl.ds(g, 1), :]
        gate = gate_ref[...]
        g_hi = gate.astype(BF16)
        g_r1 = gate - g_hi.astype(F32)
        g_mid = g_r1.astype(BF16)
        g_lo = (g_r1 - g_mid.astype(F32)).astype(BF16)

        def gather(c, carry):
            base = pl.multiple_of(c * ch, ch)
            want = (lax.broadcasted_iota(jnp.int32, (ch, tm), 0) + (base + 1)).astype(F32)
            sel = jnp.where(rrow == want, 1.0, 0.0).astype(BF16)
            xg_ref[pl.ds(base, ch), :] = _mm1(sel, xb_ref[...]).astype(BF16)
            gg_ref[pl.ds(base, ch), :] = _mm1(sel, g_hi) + _mm1(sel, g_mid) + _mm1(sel, g_lo)
            yc_ref[pl.ds(base, ch), :] = jnp.zeros((ch, D_MODEL), F32)
            return carry

        lax.fori_loop(0, n_chunks, gather, 0)

    e = g * EXPERTS_PER_GROUP + j

    def expert(c, carry):
        base = pl.multiple_of(c * ch, ch)
        xg = xg_ref[pl.ds(base, ch), :]
        hg = _mm_moe(xg, wg_ref[0, 0])
        hu = _mm_moe(xg, wu_ref[0, 0])
        gcol = jnp.sum(jnp.where(lane == e, gg_ref[pl.ds(base, ch), :], 0.0), axis=-1, keepdims=True)
        hid = hg * jax.nn.sigmoid(hg) * hu * gcol
        yc_ref[pl.ds(base, ch), :] += _mm_moe(hid, wd_ref[0, 0])
        return carry

    lax.fori_loop(0, n_chunks, expert, 0)

    @pl.when(j == EXPERTS_PER_GROUP - 1)
    def _():
        lane_t = lax.broadcasted_iota(jnp.int32, (tm, LANES), 1)
        rcol = jnp.sum(jnp.where(lane_t == g, rcol_ref[...], 0.0), axis=-1, keepdims=True)

        def scatter(c, carry):
            base = pl.multiple_of(c * ch, ch)
            want = (lax.broadcasted_iota(jnp.int32, (tm, ch), 1) + (base + 1)).astype(F32)
            sel_t = jnp.where(rcol == want, 1.0, 0.0).astype(BF16)
            y_hi, y_lo = _split(yc_ref[pl.ds(base, ch), :])
            acc_ref[...] += _mm1(sel_t, y_hi) + _mm1(sel_t, y_lo)
            return carry

        lax.fori_loop(0, n_chunks, scatter, 0)

    @pl.when((g == n_groups - 1) & (j == EXPERTS_PER_GROUP - 1))
    def _():
        o_ref[...] = _layer_norm(DEEPNORM_ALPHA * x_ref[...] + acc_ref[...], lng_ref[0, 1:2], lnb_ref[0, 1:2])


def _moe_grouped(x, ec, layer, tm):
    m = x.shape[0]
    n_groups = N_EXPERTS // EXPERTS_PER_GROUP
    lay2 = lambda shape: pl.BlockSpec((1,) + shape, lambda i, g, j: (layer,) + (0,) * len(shape))
    exp = lambda shape: pl.BlockSpec((1, 1) + shape,
                                     lambda i, g, j: (layer, g * EXPERTS_PER_GROUP + j) + (0,) * len(shape))
    const = lambda shape: pl.BlockSpec(shape, lambda i, g, j: (0,) * len(shape), pipeline_mode=pl.Buffered(1))
    tril = jnp.tril(jnp.ones((tm, tm), BF16))
    return pl.pallas_call(
        _moe_grouped_body,
        out_shape=SDS((m, D_MODEL), F32),
        grid=(m // tm, n_groups, EXPERTS_PER_GROUP),
        in_specs=[pl.BlockSpec((tm, D_MODEL), lambda i, g, j: (i, 0)),
                  const((2, D_MODEL, LANES)), const((1, LANES)), const((tm, tm)), const((tm, tm)),
                  exp((D_MODEL, D_EXPERT)), exp((D_MODEL, D_EXPERT)), exp((D_EXPERT, D_MODEL)),
                  lay2((2, D_MODEL)), lay2((2, D_MODEL))],
        out_specs=pl.BlockSpec((tm, D_MODEL), lambda i, g, j: (i, 0)),
        scratch_shapes=[pltpu.VMEM((tm, D_MODEL), BF16), pltpu.VMEM((tm, LANES), F32),
                        pltpu.VMEM((tm, LANES), F32), pltpu.VMEM((LANES, tm), F32),
                        pltpu.VMEM((tm, D_MODEL), BF16), pltpu.VMEM((tm, LANES), F32),
                        pltpu.VMEM((tm, D_MODEL), F32), pltpu.VMEM((tm, D_MODEL), F32),
                        pltpu.SMEM((n_groups,), jnp.int32)],
        compiler_params=_cparams(("parallel", "arbitrary", "arbitrary"), 56),
        name="moe_grouped",
    )(x, ec["wr"], ec["br"], tril, tril.T, ec["wg"], ec["wu"], ec["wd"], ec["lng"], ec["lnb"])


def _lane_row(v, width=LANES):
    return jnp.pad(v.astype(F32), ((0, 0), (0, width - v.shape[1])))[:, None, :]


def kernel(x_prompt, x_sample, state_ssm_re, state_ssm_im, state_conv, state_delta, cache_k, cache_v, page_table,
           w_in, b_gate, ssm_a_re, ssm_a_im, ssm_log_dt, ssm_b_re, ssm_b_im, ssm_c_re, ssm_c_im, ssm_d, ssm_w_glu,
           ssm_b_glu, dn_conv_w, dn_a_log, dn_dt_bias, dn_norm_g, da_lambda, da_norm_g, w_branch, w_out, ln_g, ln_b,
           w_router, b_router, w_gate_e, w_up_e, w_down_e):
    bp, t, _ = x_prompt.shape
    ms = x_sample.shape[0]
    depth = w_in.shape[0]
    mp = bp * t

    o_dn, o_a, o_b, o_z, o_da, o_gate = 512, 2048, 2052, 2056, 2568, 4104
    w_main = _split_hbm(jnp.concatenate([w_in[:, :, o_dn:o_a], w_in[:, :, o_da:o_gate], w_in[:, :, :o_dn],
                                         w_in[:, :, o_z:o_da]], axis=2))
    w_ab = _split_hbm(jnp.pad(w_in[:, :, o_a:o_z], ((0, 0), (0, 0), (0, LANES - 8))))
    sc = _ssm_constants(ssm_a_re, ssm_a_im, ssm_log_dt, ssm_b_re, ssm_b_im, ssm_c_re, ssm_c_im, ssm_d)
    al = _lane_row(dn_a_log)
    dtb = _lane_row(dn_dt_bias)
    dc = dict(cw=jnp.pad(dn_conv_w.astype(F32).transpose(0, 2, 1), ((0, 0), (0, 8 - CONV_W), (0, 0))),
              al=al, dtb=dtb,
              alc=jnp.broadcast_to(jnp.pad(dn_a_log.astype(F32), ((0, 0), (0, 4)))[:, :, None], (depth, 8, DN_BLOCK)),
              dtbc=jnp.broadcast_to(jnp.pad(dn_dt_bias.astype(F32), ((0, 0), (0, 4)))[:, :, None],
                                    (depth, 8, DN_BLOCK)),
              ng=dn_norm_g.astype(F32)[:, None, :])
    lam_inits = [0.8 - 0.6 * math.exp(-0.3 * l) for l in range(depth)]
    lv = da_lambda.astype(F32)
    lam = (jnp.exp(jnp.sum(lv[:, 0] * lv[:, 1], axis=-1)) - jnp.exp(jnp.sum(lv[:, 2] * lv[:, 3], axis=-1))
           + jnp.asarray(lam_inits, F32))
    ac = dict(lam=jnp.broadcast_to(lam[:, None, None], (depth, 1, LANES)), ng=da_norm_g.astype(F32)[:, None, :])
    mc = dict(wg=_split_hbm(w_in[:, :, o_gate:]), bg=b_gate.astype(F32)[:, None, :],
              wglu=_split_hbm(ssm_w_glu), bglu=ssm_b_glu.astype(F32)[:, None, :],
              wbr=_split_hbm(w_branch), wout=_split_hbm(w_out), lng=ln_g.astype(F32), lnb=ln_b.astype(F32))
    ec = dict(wr=_split_hbm(jnp.pad(w_router, ((0, 0), (0, LANES - N_EXPERTS)))),
              br=jnp.pad(b_router.astype(F32), (0, LANES - N_EXPERTS))[None, :],
              wg=w_gate_e.astype(BF16), wu=w_up_e.astype(BF16), wd=w_down_e.astype(BF16),
              lng=mc["lng"], lnb=mc["lnb"])

    n_pool = cache_k.shape[1]
    cache_k4 = cache_k.reshape(depth, n_pool, PAGE_SIZE * DA_HEADS, LANES)
    cache_v4 = cache_v.reshape(depth, n_pool, PAGE_SIZE * DA_HEADS, LANES)
    pt_flat = page_table.reshape(-1).astype(jnp.int32)

    xp = x_prompt.reshape(mp, D_MODEL)
    xs = x_sample.reshape(ms, D_MODEL)
    zero_h = jnp.zeros((bp * SSM_GROUPS, LANES), F32)
    zero_buf = jnp.zeros((bp, 8, DN_QKV), F32)
    zero_s = jnp.zeros((bp, DN_HEADS, DN_DK, DN_DK), F32)
    outs = {k: [] for k in ("p_re", "p_im", "p_conv", "p_delta",
                            "s_re", "s_im", "s_conv", "s_delta", "s_k", "s_v")}
    pkv = None
    for l in range(depth):
        u, ab = _project(xp, w_main, w_ab, l, min(1024, mp))
        ya, h_re, h_im = _ssm_prompt(u, sc, l, bp, t, zero_h)
        yb, s_fin = _dn_prompt(u, ab, dc, l, bp, t, zero_buf, zero_s)
        yc = _da_prompt(u, ac, l, bp, t, lam_inits[l])
        xm = _merge(xp, ya, yb, yc, mc, l, 256)
        xp = _moe_grouped(xm, ec, l, min(1024, mp))
        u3 = u.reshape(bp, t, N_MAIN)
        outs["p_re"].append(h_re)
        outs["p_im"].append(h_im)
        outs["p_conv"].append(u3[:, t - (CONV_W - 1):, :DN_QKV])
        outs["p_delta"].append(s_fin)
        pkv = _kv_out(u, l, depth, pkv)
        us, abs_ = _project(xs, w_main, w_ab, l, ms)
        ya_s, n_re, n_im = _ssm_step(us, sc, l, state_ssm_re[l].reshape(ms, -1).astype(F32),
                                     state_ssm_im[l].reshape(ms, -1).astype(F32))
        bufs = [state_conv[l, :, j, :].astype(F32) for j in range(CONV_W - 1)]
        yb_s, s_new = _dn_step(us, abs_, dc, l, bufs, state_delta)
        q3 = us[:, None, COL_DAQ * 512:(COL_DAQ + 1) * 512]
        k3 = us[:, None, COL_DAK * 512:(COL_DAK + 1) * 512]
        v3 = us[:, None, COL_DAV * 512:(COL_DAV + 1) * 512]
        yc_s = _da_step(q3, k3, v3, cache_k4, cache_v4, pt_flat, ac, l, lam_inits[l]).reshape(ms, 512)
        xm_s = _merge(xs, ya_s, yb_s, yc_s, mc, l, ms)
        xs = _moe(xm_s, ec, l, ms)
        outs["s_re"].append(n_re.reshape(ms, SSM_GROUPS, SSM_STATE))
        outs["s_im"].append(n_im.reshape(ms, SSM_GROUPS, SSM_STATE))
        outs["s_conv"].append(jnp.concatenate([state_conv[l, :, 1:, :].astype(F32), us[:, None, :DN_QKV]], axis=1))
        outs["s_delta"].append(s_new)
        outs["s_k"].append(k3.reshape(ms, 1, DA_HEADS, LANES))
        outs["s_v"].append(v3.reshape(ms, 1, DA_HEADS, LANES))

    st = lambda k, dt: jnp.stack(outs[k], axis=0).astype(dt)
    return (xp.reshape(bp, t, D_MODEL), xs.reshape(ms, 1, D_MODEL),
            st("p_re", state_ssm_re.dtype), st("p_im", state_ssm_im.dtype), st("p_conv", state_conv.dtype),
            st("p_delta", state_delta.dtype),
            pkv[0].reshape(depth, bp, t, DA_HEADS, LANES).astype(cache_k.dtype),
            pkv[1].reshape(depth, bp, t, DA_HEADS, LANES).astype(cache_v.dtype),
            st("s_re", state_ssm_re.dtype), st("s_im", state_ssm_im.dtype), st("s_conv", state_conv.dtype),
            st("s_delta", state_delta.dtype), st("s_k", cache_k.dtype), st("s_v", cache_v.dtype))
```

```python
import functools
import math

import jax
import jax.numpy as jnp
from jax import lax
from jax.experimental import pallas as pl
from jax.experimental.pallas import tpu as pltpu

F32 = jnp.float32
BF16 = jnp.bfloat16
SDS = jax.ShapeDtypeStruct

D_MODEL = 1024
DEPTH = 4
MIX_WIDTH = 512
N_BRANCH = 3
SSM_GROUP_CH = 16
SSM_GROUPS = 32
SSM_STATE = 64
SSM_CHUNK = 16
DN_DK = 128
DN_HEADS = 4
DN_QKV = 1536
CONV_W = 4
DN_CHUNK = 64
DN_BLOCK = 256
DA_DH = 64
DA_HEADS = 4
N_EXPERTS = 16
EXPERTS_PER_GROUP = 4
D_EXPERT = 512
PAGE_SIZE = 128
DEEPNORM_ALPHA = (2 * DEPTH) ** 0.25
LN_EPS = 1e-5
RMS_EPS = 1e-6
LANES = 128
MIB = 1024 * 1024

COL_DN = 0
COL_DAQ, COL_DAK, COL_DAV = 3, 4, 5
COL_SSM = 6
COL_Z = 7
N_MAIN = 8 * 512


def _cparams(sem, vmem_mib=48):
    return pltpu.CompilerParams(dimension_semantics=sem, vmem_limit_bytes=vmem_mib * MIB)


NN = (((1,), (0,)), ((), ()))
NT = (((1,), (1,)), ((), ()))
TN = (((0,), (0,)), ((), ()))


def _mm1(a, b, dims=NN):
    return lax.dot_general(a.astype(BF16), b.astype(BF16), dims, preferred_element_type=F32)


def _split(a):
    hi = a.astype(BF16)
    return hi, (a - hi.astype(F32)).astype(BF16)


def _mm3(a, b, dims=NN):
    ah, al = a if isinstance(a, tuple) else _split(a)
    bh, bl = b if isinstance(b, tuple) else _split(b)
    d = functools.partial(lax.dot_general, dimension_numbers=dims, preferred_element_type=F32)
    return d(ah, bh) + (d(al, bh) + d(ah, bl))


def _split_body(w_ref, o_ref):
    hi, lo = _split(w_ref[...])
    o_ref[0] = hi
    o_ref[1] = lo


def _split_hbm(w):
    shape = w.shape
    w2 = w.astype(F32).reshape(-1, shape[-1])
    rows, cols = w2.shape
    br = min(rows, 512)
    out = pl.pallas_call(
        _split_body,
        out_shape=SDS((2, rows, cols), BF16),
        grid=(rows // br,),
        in_specs=[pl.BlockSpec((br, cols), lambda i: (i, 0))],
        out_specs=pl.BlockSpec((2, br, cols), lambda i: (0, i, 0)),
        compiler_params=_cparams(("parallel",)),
        name="split",
    )(w2)
    return out.reshape((2,) + shape)


def _dot_hi(a, b):
    return jnp.dot(a, b, preferred_element_type=F32, precision=lax.Precision.HIGHEST)


_mm_ssm = _mm1
_mm_dn = _mm1
_mm_da = _mm1
_mm_moe = _mm1


def _layer_norm(x, g, b):
    mu = jnp.mean(x, axis=-1, keepdims=True)
    xc = x - mu
    var = jnp.mean(xc * xc, axis=-1, keepdims=True)
    return xc * lax.rsqrt(var + LN_EPS) * g + b


def _rms_norm(x, g):
    return x * lax.rsqrt(jnp.mean(x * x, axis=-1, keepdims=True) + RMS_EPS) * g


def _proj_body(x_ref, w_ref, wab_ref, u_ref, ab_ref, xh_ref, xl_ref):
    @pl.when(pl.program_id(1) == 0)
    def _():
        xh, xl = _split(x_ref[...])
        xh_ref[...] = xh
        xl_ref[...] = xl
        ab_ref[...] = _mm3((xh, xl), (wab_ref[0, 0], wab_ref[1, 0]))

    u_ref[...] = _mm3((xh_ref[...], xl_ref[...]), (w_ref[0, 0], w_ref[1, 0]))


def _project(x, w_main, w_ab, layer, tm):
    m = x.shape[0]
    tn = 512
    return pl.pallas_call(
        _proj_body,
        out_shape=(SDS((m, N_MAIN), F32), SDS((m, LANES), F32)),
        grid=(m // tm, N_MAIN // tn),
        in_specs=[pl.BlockSpec((tm, D_MODEL), lambda i, j: (i, 0)),
                  pl.BlockSpec((2, 1, D_MODEL, tn), lambda i, j: (0, layer, 0, j)),
                  pl.BlockSpec((2, 1, D_MODEL, LANES), lambda i, j: (0, layer, 0, 0))],
        out_specs=(pl.BlockSpec((tm, tn), lambda i, j: (i, j)),
                   pl.BlockSpec((tm, LANES), lambda i, j: (i, 0))),
        scratch_shapes=[pltpu.VMEM((tm, D_MODEL), BF16), pltpu.VMEM((tm, D_MODEL), BF16)],
        compiler_params=_cparams(("parallel", "arbitrary")),
        name="proj",
    )(x, w_main, w_ab)


def _kv_out_body(k_ref, v_ref, *refs):
    ko_ref, vo_ref = refs[-2], refs[-1]
    tm = k_ref.shape[0]
    for h in range(DA_HEADS):
        ko_ref[0, pl.ds(h, tm, stride=DA_HEADS), :] = k_ref[:, h * LANES:(h + 1) * LANES]
        vo_ref[0, pl.ds(h, tm, stride=DA_HEADS), :] = v_ref[:, h * LANES:(h + 1) * LANES]


def _kv_out(u, layer, depth, prev):
    m = u.shape[0]
    tm = min(512, m)
    shape = SDS((depth, m * DA_HEADS, LANES), F32)
    in_specs = [pl.BlockSpec((tm, 512), lambda i: (i, COL_DAK)), pl.BlockSpec((tm, 512), lambda i: (i, COL_DAV))]
    args = [u, u]
    aliases = {}
    if prev is not None:
        in_specs += [pl.BlockSpec(memory_space=pl.ANY)] * 2
        args += list(prev)
        aliases = {2: 0, 3: 1}
    out_spec = pl.BlockSpec((1, tm * DA_HEADS, LANES), lambda i: (layer, i, 0))
    return pl.pallas_call(
        _kv_out_body,
        out_shape=(shape, shape),
        grid=(m // tm,),
        in_specs=in_specs,
        out_specs=(out_spec, out_spec),
        input_output_aliases=aliases,
        compiler_params=_cparams(("parallel",)),
        name="kv_out",
    )(*args)


def _ssm_constants(a_re, a_im, log_dt, b_re, b_im, c_re, c_im, d):
    hp = lax.Precision.HIGHEST
    ell = SSM_CHUNK
    a_r, a_i = a_re.astype(F32), a_im.astype(F32)
    dt = jnp.exp(log_dt.astype(F32))[..., None]
    steps = jnp.arange(ell + 1, dtype=F32)
    mag = jnp.exp((dt * a_r)[..., None] * steps)
    ang = (dt * a_i)[..., None] * steps
    pw_r, pw_i = mag * jnp.cos(ang), mag * jnp.sin(ang)
    ab_r, ab_i = pw_r[..., 1], pw_i[..., 1]
    den = a_r * a_r + a_i * a_i
    z_r = ((ab_r - 1.0) * a_r + ab_i * a_i) / den
    z_i = (ab_i * a_r - (ab_r - 1.0) * a_i) / den
    bb_r, bb_i = b_re.astype(F32), b_im.astype(F32)
    bbar_r = z_r[..., None] * bb_r - z_i[..., None] * bb_i
    bbar_i = z_r[..., None] * bb_i + z_i[..., None] * bb_r
    c_r, c_i = c_re.astype(F32), c_im.astype(F32)
    dd, g, n, ch = bbar_r.shape
    pj_r, pj_i = pw_r[..., :ell, None], pw_i[..., :ell, None]
    pb_r = pj_r * bbar_r[:, :, :, None, :] - pj_i * bbar_i[:, :, :, None, :]
    pb_i = pj_r * bbar_i[:, :, :, None, :] + pj_i * bbar_r[:, :, :, None, :]
    kern = (jnp.einsum("dgcn,dgnjk->dgjck", c_r, pb_r, precision=hp)
            - jnp.einsum("dgcn,dgnjk->dgjck", c_i, pb_i, precision=hp))
    s_idx = jnp.arange(ell)[:, None, None]
    t_idx = jnp.arange(ell)[None, :, None]
    shift = (t_idx - s_idx == jnp.arange(ell)[None, None, :]).astype(F32)
    toep = jnp.einsum("stj,dgjck->dgsktc", shift, kern, precision=hp).reshape(dd, g, ell * ch, ell * ch)
    e_r = pb_r[:, :, :, ::-1, :].transpose(0, 1, 3, 4, 2)
    e_i = pb_i[:, :, :, ::-1, :].transpose(0, 1, 3, 4, 2)
    e_mat = jnp.concatenate([e_r, e_i], axis=-1).reshape(dd, g, ell * ch, 2 * n)
    pt_r, pt_i = pw_r[:, :, None, :, 1:ell + 1], pw_i[:, :, None, :, 1:ell + 1]
    cp_r = (c_r[..., None] * pt_r - c_i[..., None] * pt_i).transpose(0, 1, 3, 4, 2)
    cp_i = (c_r[..., None] * pt_i + c_i[..., None] * pt_r).transpose(0, 1, 3, 4, 2)
    f_mat = jnp.concatenate([cp_r, -cp_i], axis=2).reshape(dd, g, 2 * n, ell * ch)
    al_r, al_i = pw_r[..., ell], pw_i[..., ell]
    ar = jnp.concatenate([al_r, al_r], axis=-1)
    ai = jnp.concatenate([-al_i, al_i], axis=-1)
    d_t = jnp.tile(d.astype(F32).reshape(dd, g, 1, ch), (1, 1, ell, 1)).reshape(dd, g, 1, ell * ch)
    eye = jnp.eye(g, dtype=F32)
    bre = jnp.einsum("dgnk,gh->dgkhn", bbar_r, eye, precision=hp).reshape(dd, g * ch, g * n)
    bim = jnp.einsum("dgnk,gh->dgkhn", bbar_i, eye, precision=hp).reshape(dd, g * ch, g * n)
    cre = jnp.einsum("dgcn,gh->dgnhc", c_r, eye, precision=hp).reshape(dd, g * n, g * ch)
    cim = jnp.einsum("dgcn,gh->dgnhc", c_i, eye, precision=hp).reshape(dd, g * n, g * ch)
    return dict(toep=toep, e=e_mat, f=f_mat, ar=ar, ai=ai, d_t=d_t, bre=bre, bim=bim, cre=cre, cim=cim,
                a1r=ab_r.reshape(dd, 1, g * n), a1i=ab_i.reshape(dd, 1, g * n),
                d=d.astype(F32).reshape(dd, 1, g * ch))


def _ssm_state_body(u_ref, e_ref, s_ref):
    s_ref[...] = _mm_ssm(u_ref[0], e_ref[0, 0])


def _ssm_scan_body(s_ref, h0_ref, ar_ref, ai_ref, hp_ref, hf_ref):
    ar = ar_ref[...]
    ai = ai_ref[...]

    def body(c, h):
        hp_ref[c] = h
        return ar * h + ai * pltpu.roll(h, SSM_STATE, 1) + s_ref[c]

    hf_ref[...] = lax.fori_loop(0, s_ref.shape[0], body, h0_ref[...])


def _ssm_out_body(u_ref, t_ref, h_ref, f_ref, d_ref, y_ref):
    u = u_ref[0]
    y_ref[0] = _mm_ssm(u, t_ref[0, 0]) + _mm_ssm(h_ref[...], f_ref[0, 0]) + u * d_ref[0, 0]


def _ssm_prompt(u, sc, layer, bsz, t, h0):
    ell, g, ch = SSM_CHUNK, SSM_GROUPS, SSM_GROUP_CH
    nct = t // ell
    nc = bsz * nct
    w = ell * ch
    ug = u[:, COL_SSM * 512:(COL_SSM + 1) * 512].reshape(bsz, nct, ell, g, ch)
    ug = ug.transpose(3, 0, 1, 2, 4).reshape(g, nc, w)
    s = pl.pallas_call(
        _ssm_state_body,
        out_shape=SDS((nc, g * LANES), F32),
        grid=(g,),
        in_specs=[pl.BlockSpec((1, nc, w), lambda i: (i, 0, 0)),
                  pl.BlockSpec((1, 1, w, LANES), lambda i: (layer, i, 0, 0))],
        out_specs=pl.BlockSpec((nc, LANES), lambda i: (0, i)),
        compiler_params=_cparams(("parallel",)),
        name="ssm_state",
    )(ug, sc["e"])
    s_t = s.reshape(bsz, nct, g, LANES).transpose(1, 0, 2, 3).reshape(nct, bsz * g, LANES)
    rows = bsz * g
    rb = min(64, rows)
    ar = jnp.tile(sc["ar"][layer], (bsz, 1))
    ai = jnp.tile(sc["ai"][layer], (bsz, 1))
    hp, hf = pl.pallas_call(
        _ssm_scan_body,
        out_shape=(SDS((nct, rows, LANES), F32), SDS((rows, LANES), F32)),
        grid=(rows // rb,),
        in_specs=[pl.BlockSpec((nct, rb, LANES), lambda i: (0, i, 0)),
                  pl.BlockSpec((rb, LANES), lambda i: (i, 0)),
                  pl.BlockSpec((rb, LANES), lambda i: (i, 0)),
                  pl.BlockSpec((rb, LANES), lambda i: (i, 0))],
        out_specs=(pl.BlockSpec((nct, rb, LANES), lambda i: (0, i, 0)),
                   pl.BlockSpec((rb, LANES), lambda i: (i, 0))),
        compiler_params=_cparams(("parallel",)),
        name="ssm_scan",
    )(s_t, h0, ar, ai)
    hprev = hp.reshape(nct, bsz, g, LANES).transpose(1, 0, 2, 3).reshape(nc, g * LANES)
    y = pl.pallas_call(
        _ssm_out_body,
        out_shape=SDS((g, nc, w), F32),
        grid=(g,),
        in_specs=[pl.BlockSpec((1, nc, w), lambda i: (i, 0, 0)),
                  pl.BlockSpec((1, 1, w, w), lambda i: (layer, i, 0, 0)),
                  pl.BlockSpec((nc, LANES), lambda i: (0, i)),
                  pl.BlockSpec((1, 1, LANES, w), lambda i: (layer, i, 0, 0)),
                  pl.BlockSpec((1, 1, 1, w), lambda i: (layer, i, 0, 0))],
        out_specs=pl.BlockSpec((1, nc, w), lambda i: (i, 0, 0)),
        compiler_params=_cparams(("parallel",)),
        name="ssm_out",
    )(ug, sc["toep"], hprev, sc["f"], sc["d_t"])
    y = y.reshape(g, bsz, nct, ell, ch).transpose(1, 2, 3, 0, 4).reshape(bsz * t, g * ch)
    hf = hf.reshape(bsz, g, LANES)
    return y, hf[..., :SSM_STATE], hf[..., SSM_STATE:]


def _ssm_step_body(u_ref, hr_ref, hi_ref, ar_ref, ai_ref, bre_ref, bim_ref, cre_ref, cim_ref, d_ref,
                   y_ref, nr_ref, ni_ref):
    u = u_ref[...]
    hr, hi = hr_ref[...], hi_ref[...]
    ar, ai = ar_ref[0], ai_ref[0]
    nr = ar * hr - ai * hi + _mm_ssm(u, bre_ref[0])
    ni = ar * hi + ai * hr + _mm_ssm(u, bim_ref[0])
    nr_ref[...] = nr
    ni_ref[...] = ni
    y_ref[...] = _mm_ssm(nr, cre_ref[0]) - _mm_ssm(ni, cim_ref[0]) + u * d_ref[0]


def _ssm_step(u, sc, layer, h_re, h_im):
    ms = u.shape[0]
    gn = SSM_GROUPS * SSM_STATE
    full = lambda shape: pl.BlockSpec(shape, lambda i: (0,) * len(shape))
    lay = lambda shape: pl.BlockSpec((1,) + shape, lambda i: (layer,) + (0,) * len(shape))
    return pl.pallas_call(
        _ssm_step_body,
        out_shape=(SDS((ms, 512), F32), SDS((ms, gn), F32), SDS((ms, gn), F32)),
        grid=(1,),
        in_specs=[pl.BlockSpec((ms, 512), lambda i: (0, COL_SSM)), full((ms, gn)), full((ms, gn)),
                  lay((1, gn)), lay((1, gn)), lay((512, gn)), lay((512, gn)), lay((gn, 512)), lay((gn, 512)),
                  lay((1, 512))],
        out_specs=(full((ms, 512)), full((ms, gn)), full((ms, gn))),
        compiler_params=_cparams(("arbitrary",)),
        name="ssm_step",
    )(u, h_re, h_im, sc["a1r"], sc["a1i"], sc["bre"], sc["bim"], sc["cre"], sc["cim"], sc["d"])


def _shift_rows(x, prev8, s):
    if s == 0:
        return x
    xs = pltpu.roll(x, s, 0)
    ps = pltpu.roll(prev8, s, 0)
    row = lax.broadcasted_iota(jnp.int32, ps.shape, 0)
    head = jnp.where(row < s, ps, xs[:8])
    return jnp.concatenate([head, xs[8:]], axis=0)


def _dn_prompt_body(x_ref, prev_ref, buf_ref, ab_ref, abt_ref, z_ref, s0_ref, cw_ref, al_ref, dtb_ref,
                    alc_ref, dtbc_ref, ng_ref, y_ref, sf_ref, s_scr):
    i = pl.program_id(1)
    nb = pl.num_programs(1)
    tb = DN_BLOCK
    nck = tb // DN_CHUNK

    @pl.when(i == 0)
    def _():
        s_scr[...] = s0_ref[0]

    x = x_ref[...]
    prev8 = jnp.where(i == 0, buf_ref[0], prev_ref[...])
    cw = cw_ref[0]
    conv = x * cw[3:4]
    for j in range(CONV_W - 1):
        conv = conv + _shift_rows(x, prev8, CONV_W - 1 - j) * cw[j:j + 1]
    y = conv * jax.nn.sigmoid(conv)

    ab = ab_ref[...]
    gmat = -jnp.exp(al_ref[0]) * jax.nn.softplus(ab + dtb_ref[0])
    beta_m = jax.nn.sigmoid(ab)
    abt = abt_ref[0]
    g_rows = -jnp.exp(alc_ref[0]) * jax.nn.softplus(abt + dtbc_ref[0])

    r = lax.broadcasted_iota(jnp.int32, (tb, tb), 0)
    c = lax.broadcasted_iota(jnp.int32, (tb, tb), 1)
    same = (r // DN_CHUNK) == (c // DN_CHUNK)
    incl = same & (r >= c)
    strict = same & (r > c)
    l_incl = jnp.where(incl, 1.0, 0.0).astype(F32)
    u_incl = jnp.where(same & (c >= r), 1.0, 0.0).astype(F32)
    blk1 = jnp.where(same, 1.0, 0.0).astype(F32)
    cum_c = _dot_hi(l_incl, gmat)
    tot_c = _dot_hi(blk1, gmat)
    cum_r = _dot_hi(g_rows, u_incl)
    rowi = lax.broadcasted_iota(jnp.int32, (tb, 1), 0)

    heads = range(DN_HEADS)
    q, k, v, beta, gam, decay, tot, cc = [], [], [], [], [], [], [], []
    for h in heads:
        qh = y[:, h * DN_DK:(h + 1) * DN_DK]
        kh = y[:, 512 + h * DN_DK:512 + (h + 1) * DN_DK]
        q.append(qh * lax.rsqrt(jnp.sum(qh * qh, axis=-1, keepdims=True) + RMS_EPS) * (DN_DK ** -0.5))
        k.append(kh * lax.rsqrt(jnp.sum(kh * kh, axis=-1, keepdims=True) + RMS_EPS))
        v.append(y[:, 1024 + h * DN_DK:1024 + (h + 1) * DN_DK])
        cc.append(cum_c[:, h:h + 1])
        tot.append(tot_c[:, h:h + 1])
        beta.append(beta_m[:, 4 + h:5 + h])
        gam.append(jnp.exp(cc[h]))
        decay.append(jnp.where(incl, jnp.exp(jnp.where(incl, cc[h] - cum_r[h:h + 1, :], 0.0)), 0.0))
    kk = [_mm_dn(k[h], k[h], NT) for h in heads]
    qk = [_mm_dn(q[h], k[h], NT) for h in heads]
    p_mat = [jnp.where(strict, beta[h] * decay[h] * kk[h], 0.0) for h in heads]
    n_mat = [-p_mat[h] for h in heads]
    for _ in range(5):
        p_mat = [_mm_dn(p_mat[h], p_mat[h]) for h in heads]
        n_mat = [n_mat[h] + p_mat[h] + _mm_dn(n_mat[h], p_mat[h]) for h in heads]
    rhs = [jnp.concatenate([beta[h] * v[h], (beta[h] * gam[h]) * k[h]], axis=1) for h in heads]
    sol = [rhs[h] + _mm_dn(n_mat[h], rhs[h]) for h in heads]
    qk = [jnp.where(incl, qk[h] * decay[h], 0.0) for h in heads]
    qg = [q[h] * gam[h] for h in heads]
    k_dec = [k[h] * jnp.exp(tot[h] - cc[h]) for h in heads]
    e_tot = [jnp.exp(tot[h]) for h in heads]
    s = [s_scr[h] for h in heads]
    o_inter = [[] for _ in heads]
    ws = [[] for _ in heads]
    for ck in range(nck):
        lo, hi = ck * DN_CHUNK, (ck + 1) * DN_CHUNK
        w_c = [sol[h][lo:hi, :DN_DK] - _mm_dn(sol[h][lo:hi, DN_DK:], s[h]) for h in heads]
        for h in heads:
            o_inter[h].append(_mm_dn(qg[h][lo:hi], s[h]))
            ws[h].append(w_c[h])
        s = [e_tot[h][lo:lo + 1] * s[h] + _mm_dn(k_dec[h][lo:hi], w_c[h], TN) for h in heads]
    outs = []
    for h in heads:
        s_scr[h] = s[h]
        o = jnp.concatenate(o_inter[h], axis=0) + _mm_dn(qk[h], jnp.concatenate(ws[h], axis=0))
        zh = z_ref[:, h * DN_DK:(h + 1) * DN_DK]
        outs.append(_rms_norm(o, ng_ref[0]) * (zh * jax.nn.sigmoid(zh)))
    y_ref[...] = jnp.concatenate(outs, axis=1)

    @pl.when(i == nb - 1)
    def _():
        sf_ref[0] = s_scr[...]


def _dn_prompt(u, ab, dc, layer, bsz, t, conv_buf8, s0):
    tb = DN_BLOCK
    nb = t // tb
    abt = ab[:, :8].reshape(bsz, t, 8).transpose(0, 2, 1)
    lay = lambda shape: pl.BlockSpec((1,) + shape, lambda b, i: (layer,) + (0,) * len(shape))
    return pl.pallas_call(
        _dn_prompt_body,
        out_shape=(SDS((bsz * t, 512), F32), SDS((bsz, DN_HEADS, DN_DK, DN_DK), F32)),
        grid=(bsz, nb),
        in_specs=[pl.BlockSpec((tb, DN_QKV), lambda b, i: (b * nb + i, 0)),
                  pl.BlockSpec((8, DN_QKV), lambda b, i: (jnp.maximum((b * nb + i) * (tb // 8) - 1, 0), 0)),
                  pl.BlockSpec((1, 8, DN_QKV), lambda b, i: (b, 0, 0)),
                  pl.BlockSpec((tb, LANES), lambda b, i: (b * nb + i, 0)),
                  pl.BlockSpec((1, 8, tb), lambda b, i: (b, 0, i)),
                  pl.BlockSpec((tb, 512), lambda b, i: (b * nb + i, COL_Z)),
                  pl.BlockSpec((1, DN_HEADS, DN_DK, DN_DK), lambda b, i: (b, 0, 0, 0)),
                  lay((8, DN_QKV)), lay((1, LANES)), lay((1, LANES)), lay((8, tb)), lay((8, tb)),
                  lay((1, LANES))],
        out_specs=(pl.BlockSpec((tb, 512), lambda b, i: (b * nb + i, 0)),
                   pl.BlockSpec((1, DN_HEADS, DN_DK, DN_DK), lambda b, i: (b, 0, 0, 0))),
        scratch_shapes=[pltpu.VMEM((DN_HEADS, DN_DK, DN_DK), F32)],
        compiler_params=_cparams(("parallel", "arbitrary")),
        name="dn_prompt",
    )(u, u, conv_buf8, ab, abt, u, s0, dc["cw"], dc["al"], dc["dtb"], dc["alc"], dc["dtbc"], dc["ng"])


def _dn_step_body(x_ref, b0_ref, b1_ref, b2_ref, ab_ref, z_ref, s0_ref, cw_ref, al_ref, dtb_ref, ng_ref,
                  y_ref, sn_ref):
    cw = cw_ref[0]
    conv = b0_ref[...] * cw[0:1] + b1_ref[...] * cw[1:2] + b2_ref[...] * cw[2:3] + x_ref[...] * cw[3:4]
    y = conv * jax.nn.sigmoid(conv)
    ab = ab_ref[...]
    gam_m = jnp.exp(-jnp.exp(al_ref[0]) * jax.nn.softplus(ab + dtb_ref[0]))
    beta_m = jax.nn.sigmoid(ab)
    nrow = x_ref.shape[0]
    row = lax.broadcasted_iota(jnp.int32, (nrow, DN_DK), 0)
    outs = []
    for h in range(DN_HEADS):
        qh = y[:, h * DN_DK:(h + 1) * DN_DK]
        kh = y[:, 512 + h * DN_DK:512 + (h + 1) * DN_DK]
        vh = y[:, 1024 + h * DN_DK:1024 + (h + 1) * DN_DK]
        qh = qh * lax.rsqrt(jnp.sum(qh * qh, axis=-1, keepdims=True) + RMS_EPS) * (DN_DK ** -0.5)
        kh = kh * lax.rsqrt(jnp.sum(kh * kh, axis=-1, keepdims=True) + RMS_EPS)
        gam = gam_m[:, h:h + 1]
        beta = beta_m[:, 4 + h:5 + h]
        ks = jnp.zeros((nrow, DN_DK), F32)
        qs = jnp.zeros((nrow, DN_DK), F32)
        for n in range(nrow):
            s_n = s0_ref[0, n, h]
            ks = jnp.where(row == n, _mm_dn(kh, s_n), ks)
            qs = jnp.where(row == n, _mm_dn(qh, s_n), qs)
        w = beta * vh - (beta * gam) * ks
        o = gam * qs + jnp.sum(qh * kh, axis=-1, keepdims=True) * w
        for n in range(nrow):
            k_only = jnp.where(row == n, kh, 0.0)
            sn_ref[n, h] = gam[n:n + 1] * s0_ref[0, n, h] + _mm_dn(k_only, w, TN)
        zh = z_ref[:, h * DN_DK:(h + 1) * DN_DK]
        outs.append(_rms_norm(o, ng_ref[0]) * (zh * jax.nn.sigmoid(zh)))
    y_ref[...] = jnp.concatenate(outs, axis=1)


def _dn_step(u, ab, dc, layer, bufs, s0_all):
    ms = u.shape[0]
    nr = 8
    lay = lambda shape: pl.BlockSpec((1,) + shape, lambda i: (layer,) + (0,) * len(shape))
    rows = lambda w: pl.BlockSpec((nr, w), lambda i: (i, 0))
    return pl.pallas_call(
        _dn_step_body,
        out_shape=(SDS((ms, 512), F32), SDS((ms, DN_HEADS, DN_DK, DN_DK), F32)),
        grid=(ms // nr,),
        in_specs=[rows(DN_QKV), rows(DN_QKV), rows(DN_QKV), rows(DN_QKV), rows(LANES),
                  pl.BlockSpec((nr, 512), lambda i: (i, COL_Z)),
                  pl.BlockSpec((1, nr, DN_HEADS, DN_DK, DN_DK), lambda i: (layer, i, 0, 0, 0)),
                  lay((8, DN_QKV)), lay((1, LANES)), lay((1, LANES)), lay((1, LANES))],
        out_specs=(rows(512), pl.BlockSpec((nr, DN_HEADS, DN_DK, DN_DK), lambda i: (i, 0, 0, 0))),
        compiler_params=_cparams(("parallel",)),
        name="dn_step",
    )(u, bufs[0], bufs[1], bufs[2], ab, u, s0_all, dc["cw"], dc["al"], dc["dtb"], dc["ng"])


def _da_prompt_body(q_ref, k_ref, v_ref, lam_ref, ng_ref, o_ref, qm_ref, m_ref, l_ref, acc_ref, *, lam_init):
    qi = pl.program_id(1)
    ki = pl.program_id(2)
    tq = q_ref.shape[0]
    tk = k_ref.shape[0]

    @pl.when(ki == 0)
    def _():
        lane = lax.broadcasted_iota(jnp.int32, (tq, LANES), 1)
        for h in range(DA_HEADS):
            qh = q_ref[:, h * LANES:(h + 1) * LANES] * (DA_DH ** -0.5)
            qm_ref[2 * h] = jnp.where(lane < DA_DH, qh, 0.0).astype(BF16)
            qm_ref[2 * h + 1] = jnp.where(lane >= DA_DH, qh, 0.0).astype(BF16)
        m_ref[...] = jnp.full(m_ref.shape, -jnp.inf, F32)
        l_ref[...] = jnp.zeros(l_ref.shape, F32)
        acc_ref[...] = jnp.zeros(acc_ref.shape, F32)

    def step(diagonal):
        kb = k_ref[...].astype(BF16)
        vb = v_ref[...].astype(BF16)
        if diagonal:
            r = lax.broadcasted_iota(jnp.int32, (tk, tq), 0)
            c = lax.broadcasted_iota(jnp.int32, (tk, tq), 1)
            keep = r <= c
        for h in range(DA_HEADS):
            kh = kb[:, h * LANES:(h + 1) * LANES]
            vh = vb[:, h * LANES:(h + 1) * LANES]
            for mp in range(2):
                idx = 2 * h + mp
                s = _mm_da(kh, qm_ref[idx], NT)
                if diagonal:
                    s = jnp.where(keep, s, -jnp.inf)
                m_prev = m_ref[idx:idx + 1, :]
                m_new = jnp.maximum(m_prev, jnp.max(s, axis=0, keepdims=True))
                alpha = jnp.exp(m_prev - m_new)
                p = jnp.exp(s - m_new)
                l_ref[idx:idx + 1, :] = alpha * l_ref[idx:idx + 1, :] + jnp.sum(p, axis=0, keepdims=True)
                acc_ref[idx] = alpha * acc_ref[idx] + _mm_da(vh, p, TN)
                m_ref[idx:idx + 1, :] = m_new

    @pl.when(ki < qi)
    def _():
        step(False)

    @pl.when(ki == qi)
    def _():
        step(True)
        lam = lam_ref[0][:, 0:1]
        for h in range(DA_HEADS):
            o_t = (acc_ref[2 * h] / l_ref[2 * h:2 * h + 1, :]
                   - lam * (acc_ref[2 * h + 1] / l_ref[2 * h + 1:2 * h + 2, :]))
            o_ref[:, h * LANES:(h + 1) * LANES] = _rms_norm(o_t.T, ng_ref[0]) * (1.0 - lam_init)


def _da_prompt(u, ac, layer, bsz, t, lam_init):
    tq = 512
    nq = t // tq
    lay = lambda shape: pl.BlockSpec((1,) + shape, lambda b, i, j: (layer,) + (0,) * len(shape))
    return pl.pallas_call(
        functools.partial(_da_prompt_body, lam_init=lam_init),
        out_shape=SDS((bsz * t, 512), F32),
        grid=(bsz, nq, nq),
        in_specs=[pl.BlockSpec((tq, 512), lambda b, i, j: (b * nq + i, COL_DAQ)),
                  pl.BlockSpec((tq, 512), lambda b, i, j: (b * nq + jnp.minimum(i, j), COL_DAK)),
                  pl.BlockSpec((tq, 512), lambda b, i, j: (b * nq + jnp.minimum(i, j), COL_DAV)),
                  lay((1, LANES)), lay((1, LANES))],
        out_specs=pl.BlockSpec((tq, 512), lambda b, i, j: (b * nq + i, 0)),
        scratch_shapes=[pltpu.VMEM((2 * DA_HEADS, tq, LANES), BF16),
                        pltpu.VMEM((2 * DA_HEADS, tq), F32),
                        pltpu.VMEM((2 * DA_HEADS, tq), F32),
                        pltpu.VMEM((2 * DA_HEADS, LANES, tq), F32)],
        compiler_params=_cparams(("parallel", "parallel", "arbitrary")),
        name="da_prompt",
    )(u, u, u, ac["lam"], ac["ng"])


def _da_step_body(pt_ref, q_ref, kn_ref, vn_ref, lam_ref, ng_ref, *refs, n_pages, lam_init):
    k_pages = refs[:n_pages]
    v_pages = refs[n_pages:2 * n_pages]
    o_ref = refs[2 * n_pages]
    s_scr = refs[2 * n_pages + 1]
    nh = DA_HEADS
    ps = PAGE_SIZE
    row = lax.broadcasted_iota(jnp.int32, (LANES, LANES), 0)
    lane = lax.broadcasted_iota(jnp.int32, (LANES, LANES), 1)
    q = q_ref[0] * (DA_DH ** -0.5)
    qmats = []
    for h in range(nh):
        qh = jnp.broadcast_to(q[:, h * LANES:(h + 1) * LANES], (LANES, LANES))
        pick = ((row == 2 * h) & (lane < DA_DH)) | ((row == 2 * h + 1) & (lane >= DA_DH))
        qmats.append(jnp.where(pick, qh, 0.0).astype(BF16))
    for j in range(n_pages):
        acc = jnp.zeros((ps, LANES), F32)
        for h in range(nh):
            kh = k_pages[j][pl.ds(h, ps, stride=nh), :].astype(BF16)
            acc = acc + _mm1(kh, qmats[h], NT)
        s_scr[j * ps:(j + 1) * ps, :] = acc
    accn = jnp.zeros((8, LANES), F32)
    kn = kn_ref[0]
    for h in range(nh):
        kh = jnp.broadcast_to(kn[:, h * LANES:(h + 1) * LANES], (8, LANES)).astype(BF16)
        accn = accn + _mm1(kh, qmats[h], NT)
    row8 = lax.broadcasted_iota(jnp.int32, (8, LANES), 0)
    s_scr[n_pages * ps:n_pages * ps + 8, :] = jnp.where(row8 == 0, accn, -jnp.inf)

    s = s_scr[...]
    m = jnp.max(s, axis=0, keepdims=True)
    p = jnp.exp(s - m)
    pn = (p / jnp.sum(p, axis=0, keepdims=True)).astype(BF16)
    er = lax.broadcasted_iota(jnp.int32, (LANES, nh * LANES), 0)
    ec = lax.broadcasted_iota(jnp.int32, (LANES, nh * LANES), 1) // LANES
    x1 = jnp.where(er == 2 * ec, 1.0, 0.0).astype(BF16)
    x2 = jnp.where(er == 2 * ec + 1, 1.0, 0.0).astype(BF16)
    lam = lam_ref[0][:, 0:1]
    accs = [jnp.zeros((ps, LANES), F32) for _ in range(nh)]
    for j in range(n_pages):
        pj = pn[j * ps:(j + 1) * ps]
        cj = _mm1(pj, x1) - lam * _mm1(pj, x2)
        for h in range(nh):
            vh = v_pages[j][pl.ds(h, ps, stride=nh), :]
            accs[h] = accs[h] + cj[:, h * LANES:(h + 1) * LANES] * vh
    pj = pn[n_pages * ps:n_pages * ps + 8]
    cn = _mm1(pj, x1) - lam * _mm1(pj, x2)
    vn = vn_ref[0]
    for h in range(nh):
        o = jnp.sum(accs[h], axis=0, keepdims=True)
        o = o + jnp.sum(cn[:, h * LANES:(h + 1) * LANES] * vn[:, h * LANES:(h + 1) * LANES], axis=0, keepdims=True)
        o_ref[0, :, h * LANES:(h + 1) * LANES] = _rms_norm(o, ng_ref[0]) * (1.0 - lam_init)


def _da_step(q3, k3, v3, cache_k4, cache_v4, pt_flat, ac, layer, lam_init):
    ms = q3.shape[0]
    n_pages = pt_flat.shape[0] // ms
    rows = PAGE_SIZE * DA_HEADS
    tok = pl.BlockSpec((1, 1, 512), lambda b, pt: (b, 0, 0))
    lay = pl.BlockSpec((1, 1, LANES), lambda b, pt: (layer, 0, 0))

    def page_spec(j):
        return pl.BlockSpec((None, None, rows, LANES), lambda b, pt: (layer, pt[b * n_pages + j], 0, 0))

    grid_spec = pltpu.PrefetchScalarGridSpec(
        num_scalar_prefetch=1,
        grid=(ms,),
        in_specs=[tok, tok, tok, lay, lay] + [page_spec(j) for j in range(n_pages)]
        + [page_spec(j) for j in range(n_pages)],
        out_specs=tok,
        scratch_shapes=[pltpu.VMEM((n_pages * PAGE_SIZE + 8, LANES), F32)],
    )
    return pl.pallas_call(
        functools.partial(_da_step_body, n_pages=n_pages, lam_init=lam_init),
        out_shape=SDS((ms, 1, 512), F32),
        grid_spec=grid_spec,
        compiler_params=_cparams(("arbitrary",), 56),
        name="da_step",
    )(pt_flat, q3, k3, v3, ac["lam"], ac["ng"], *([cache_k4] * n_pages), *([cache_v4] * n_pages))


def _merge_body(x_ref, ya_ref, yb_ref, yc_ref, wg_ref, bg_ref, wglu_ref, bglu_ref, wbr_ref, wout_ref,
                lng_ref, lnb_ref, o_ref):
    x = x_ref[...]
    xs = _split(x)
    ya = jax.nn.gelu(ya_ref[...])
    ya = ya * jax.nn.sigmoid(_mm3(ya, (wglu_ref[0, 0], wglu_ref[1, 0])) + bglu_ref[0])
    ys = (ya, yb_ref[...], yc_ref[...])
    merged = jnp.zeros(x.shape, F32)
    for i in range(N_BRANCH):
        lo, hi = i * D_MODEL, (i + 1) * D_MODEL
        gate = jax.nn.sigmoid(_mm3(xs, (wg_ref[0, 0, :, lo:hi], wg_ref[1, 0, :, lo:hi])) + bg_ref[0, :, lo:hi])
        merged = merged + gate * _mm3(ys[i], (wbr_ref[0, 0, i], wbr_ref[1, 0, i]))
    mix = _mm3(merged, (wout_ref[0, 0], wout_ref[1, 0]))
    o_ref[...] = _layer_norm(DEEPNORM_ALPHA * x + mix, lng_ref[0, 0:1], lnb_ref[0, 0:1])


def _merge(x, ya, yb, yc, mc, layer, tm):
    m = x.shape[0]
    lay = lambda shape: pl.BlockSpec((1,) + shape, lambda i: (layer,) + (0,) * len(shape))
    wlay = lambda shape: pl.BlockSpec((2, 1) + shape, lambda i: (0, layer) + (0,) * len(shape),
                                      pipeline_mode=pl.Buffered(1))
    tok = lambda w: pl.BlockSpec((tm, w), lambda i: (i, 0))
    return pl.pallas_call(
        _merge_body,
        out_shape=SDS((m, D_MODEL), F32),
        grid=(m // tm,),
        in_specs=[tok(D_MODEL), tok(512), tok(512), tok(512),
                  wlay((D_MODEL, N_BRANCH * D_MODEL)), lay((1, N_BRANCH * D_MODEL)), wlay((512, 512)), lay((1, 512)),
                  wlay((N_BRANCH, 512, D_MODEL)), wlay((D_MODEL, D_MODEL)), lay((2, D_MODEL)), lay((2, D_MODEL))],
        out_specs=tok(D_MODEL),
        compiler_params=_cparams(("parallel",), 56),
        name="merge",
    )(x, ya, yb, yc, mc["wg"], mc["bg"], mc["wglu"], mc["bglu"], mc["wbr"], mc["wout"], mc["lng"], mc["lnb"])


def _router_gate(logits, b_router):
    lane = lax.broadcasted_iota(jnp.int32, logits.shape, 1)
    valid = lane < N_EXPERTS
    lg = jnp.where(valid, logits, -jnp.inf)
    ex = jnp.exp(lg - jnp.max(lg, axis=-1, keepdims=True))
    scores = ex / jnp.sum(ex, axis=-1, keepdims=True)
    neg = -1e30
    sel = jnp.where(valid, scores + b_router, neg)
    pos = lane % EXPERTS_PER_GROUP
    rank = jnp.zeros(logits.shape, F32)
    for s in range(1, EXPERTS_PER_GROUP):
        lo = pltpu.roll(sel, s, 1)
        hi = pltpu.roll(sel, LANES - s, 1)
        rank = rank + jnp.where((pos >= s) & (lo >= sel), 1.0, 0.0)
        rank = rank + jnp.where((pos + s < EXPERTS_PER_GROUP) & (hi > sel), 1.0, 0.0)
    top2 = rank < 2.0
    msel = jnp.where(top2, sel, 0.0)
    gs = msel
    for s in range(1, EXPERTS_PER_GROUP):
        gs = gs + jnp.where(pos >= s, pltpu.roll(msel, s, 1), 0.0)
        gs = gs + jnp.where(pos + s < EXPERTS_PER_GROUP, pltpu.roll(msel, LANES - s, 1), 0.0)
    gs = jnp.where(valid, gs, neg)
    beaten = jnp.zeros(logits.shape, F32)
    for s in range(EXPERTS_PER_GROUP, N_EXPERTS, EXPERTS_PER_GROUP):
        lo = pltpu.roll(gs, s, 1)
        hi = pltpu.roll(gs, LANES - s, 1)
        beaten = beaten + jnp.where((lane >= s) & (lo >= gs), 1.0, 0.0)
        beaten = beaten + jnp.where((lane + s < N_EXPERTS) & (hi > gs), 1.0, 0.0)
    keep = top2 & (beaten < 0.5) & valid
    w = jnp.where(keep, scores, 0.0)
    return w / jnp.sum(w, axis=-1, keepdims=True), jnp.where(keep, 1.0, 0.0)


def _moe_body(x_ref, wr_ref, br_ref, wg_ref, wu_ref, wd_ref, lng_ref, lnb_ref, o_ref, xb_ref, gate_ref, acc_ref):
    e = pl.program_id(1)

    @pl.when(e == 0)
    def _():
        xb_ref[...] = x_ref[...].astype(BF16)
        gate_ref[...] = _router_gate(_mm3(x_ref[...], (wr_ref[0], wr_ref[1])), br_ref[...])[0]
        acc_ref[...] = jnp.zeros(acc_ref.shape, F32)

    xb = xb_ref[...]
    hg = _mm_moe(xb, wg_ref[0, 0])
    hu = _mm_moe(xb, wu_ref[0, 0])
    gate = gate_ref[...]
    lane = lax.broadcasted_iota(jnp.int32, gate.shape, 1)
    gcol = jnp.sum(jnp.where(lane == e, gate, 0.0), axis=-1, keepdims=True)
    hid = hg * jax.nn.sigmoid(hg) * hu * gcol
    acc_ref[...] += _mm_moe(hid, wd_ref[0, 0])

    @pl.when(e == N_EXPERTS - 1)
    def _():
        o_ref[...] = _layer_norm(DEEPNORM_ALPHA * x_ref[...] + acc_ref[...], lng_ref[0, 1:2], lnb_ref[0, 1:2])


def _moe(x, ec, layer, tm):
    m = x.shape[0]
    lay2 = lambda shape: pl.BlockSpec((1,) + shape, lambda i, e: (layer,) + (0,) * len(shape))
    exp = lambda shape: pl.BlockSpec((1, 1) + shape, lambda i, e: (layer, e) + (0,) * len(shape))
    return pl.pallas_call(
        _moe_body,
        out_shape=SDS((m, D_MODEL), F32),
        grid=(m // tm, N_EXPERTS),
        in_specs=[pl.BlockSpec((tm, D_MODEL), lambda i, e: (i, 0)),
                  pl.BlockSpec((2, D_MODEL, LANES), lambda i, e: (0, 0, 0)),
                  pl.BlockSpec((1, LANES), lambda i, e: (0, 0)),
                  exp((D_MODEL, D_EXPERT)), exp((D_MODEL, D_EXPERT)), exp((D_EXPERT, D_MODEL)),
                  lay2((2, D_MODEL)), lay2((2, D_MODEL))],
        out_specs=pl.BlockSpec((tm, D_MODEL), lambda i, e: (i, 0)),
        scratch_shapes=[pltpu.VMEM((tm, D_MODEL), BF16), pltpu.VMEM((tm, LANES), F32),
                        pltpu.VMEM((tm, D_MODEL), F32)],
        compiler_params=_cparams(("parallel", "arbitrary")),
        name="moe",
    )(x, ec["wr"], ec["br"], ec["wg"], ec["wu"], ec["wd"], ec["lng"], ec["lnb"])


MOE_CHUNK = 512


def _moe_grouped_body(x_ref, wr_ref, br_ref, tril_ref, triu_ref, wg_ref, wu_ref, wd_ref, lng_ref, lnb_ref, o_ref,
                      xb_ref, gate_ref, rcol_ref, rrow_ref, xg_ref, gg_ref, yc_ref, acc_ref, cnt_ref):
    g = pl.program_id(1)
    j = pl.program_id(2)
    tm = x_ref.shape[0]
    ch = MOE_CHUNK
    n_groups = N_EXPERTS // EXPERTS_PER_GROUP

    @pl.when((g == 0) & (j == 0))
    def _():
        x = x_ref[...]
        xb_ref[...] = x.astype(BF16)
        gate, keep = _router_gate(_mm3(x, (wr_ref[0], wr_ref[1])), br_ref[...])
        gate_ref[...] = gate
        er = lax.broadcasted_iota(jnp.int32, (LANES, LANES), 0)
        ec = lax.broadcasted_iota(jnp.int32, (LANES, LANES), 1)
        to_group = jnp.where((er // EXPERTS_PER_GROUP == ec) & (er < N_EXPERTS), 0.5, 0.0)
        member = _mm1(keep, to_group)
        rank_c = _mm1(tril_ref[...], member)
        rcol_ref[...] = jnp.where(member > 0.5, rank_c, 0.0)
        member_t = member.T
        rank_r = _mm1(member_t, triu_ref[...])
        rrow_ref[...] = jnp.where(member_t > 0.5, rank_r, 0.0)
        for gi in range(n_groups):
            cnt_ref[gi] = jnp.sum(member[:, gi:gi + 1]).astype(jnp.int32)
        acc_ref[...] = jnp.zeros(acc_ref.shape, F32)

    n_chunks = (cnt_ref[g] + (ch - 1)) // ch
    lane = lax.broadcasted_iota(jnp.int32, (ch, LANES), 1)

    @pl.when(j == 0)
    def _():
        rrow = rrow_ref[pl.ds(g, 1), :]
        gate = gate_ref[...]
        g_hi = gate.astype(BF16)
        g_r1 = gate - g_hi.astype(F32)
        g_mid = g_r1.astype(BF16)
        g_lo = (g_r1 - g_mid.astype(F32)).astype(BF16)

        def gather(c, carry):
            base = pl.multiple_of(c * ch, ch)
            want = (lax.broadcasted_iota(jnp.int32, (ch, tm), 0) + (base + 1)).astype(F32)
            sel = jnp.where(rrow == want, 1.0, 0.0).astype(BF16)
            xg_ref[pl.ds(base, ch), :] = _mm1(sel, xb_ref[...]).astype(BF16)
            gg_ref[pl.ds(base, ch), :] = _mm1(sel, g_hi) + _mm1(sel, g_mid) + _mm1(sel, g_lo)
            yc_ref[pl.ds(base, ch), :] = jnp.zeros((ch, D_MODEL), F32)
            return carry

        lax.fori_loop(0, n_chunks, gather, 0)

    e = g * EXPERTS_PER_GROUP + j

    def expert(c, carry):
        base = pl.multiple_of(c * ch, ch)
        xg = xg_ref[pl.ds(base, ch), :]
        hg = _mm_moe(xg, wg_ref[0, 0])
        hu = _mm_moe(xg, wu_ref[0, 0])
        gcol = jnp.sum(jnp.where(lane == e, gg_ref[pl.ds(base, ch), :], 0.0), axis=-1, keepdims=True)
        hid = hg * jax.nn.sigmoid(hg) * hu * gcol
        yc_ref[pl.ds(base, ch), :] += _mm_moe(hid, wd_ref[0, 0])
        return carry

    lax.fori_loop(0, n_chunks, expert, 0)

    @pl.when(j == EXPERTS_PER_GROUP - 1)
    def _():
        lane_t = lax.broadcasted_iota(jnp.int32, (tm, LANES), 1)
        rcol = jnp.sum(jnp.where(lane_t == g, rcol_ref[...], 0.0), axis=-1, keepdims=True)

        def scatter(c, carry):
            base = pl.multiple_of(c * ch, ch)
            want = (lax.broadcasted_iota(jnp.int32, (tm, ch), 1) + (base + 1)).astype(F32)
            sel_t = jnp.where(rcol == want, 1.0, 0.0).astype(BF16)
            y_hi, y_lo = _split(yc_ref[pl.ds(base, ch), :])
            acc_ref[...] += _mm1(sel_t, y_hi) + _mm1(sel_t, y_lo)
            return carry

        lax.fori_loop(0, n_chunks, scatter, 0)

    @pl.when((g == n_groups - 1) & (j == EXPERTS_PER_GROUP - 1))
    def _():
        o_ref[...] = _layer_norm(DEEPNORM_ALPHA * x_ref[...] + acc_ref[...], lng_ref[0, 1:2], lnb_ref[0, 1:2])


def _moe_grouped(x, ec, layer, tm):
    m = x.shape[0]
    n_groups = N_EXPERTS // EXPERTS_PER_GROUP
    lay2 = lambda shape: pl.BlockSpec((1,) + shape, lambda i, g, j: (layer,) + (0,) * len(shape))
    exp = lambda shape: pl.BlockSpec((1, 1) + shape,
                                     lambda i, g, j: (layer, g * EXPERTS_PER_GROUP + j) + (0,) * len(shape))
    const = lambda shape: pl.BlockSpec(shape, lambda i, g, j: (0,) * len(shape), pipeline_mode=pl.Buffered(1))
    tril = jnp.tril(jnp.ones((tm, tm), BF16))
    return pl.pallas_call(
        _moe_grouped_body,
        out_shape=SDS((m, D_MODEL), F32),
        grid=(m // tm, n_groups, EXPERTS_PER_GROUP),
        in_specs=[pl.BlockSpec((tm, D_MODEL), lambda i, g, j: (i, 0)),
                  const((2, D_MODEL, LANES)), const((1, LANES)), const((tm, tm)), const((tm, tm)),
                  exp((D_MODEL, D_EXPERT)), exp((D_MODEL, D_EXPERT)), exp((D_EXPERT, D_MODEL)),
                  lay2((2, D_MODEL)), lay2((2, D_MODEL))],
        out_specs=pl.BlockSpec((tm, D_MODEL), lambda i, g, j: (i, 0)),
        scratch_shapes=[pltpu.VMEM((tm, D_MODEL), BF16), pltpu.VMEM((tm, LANES), F32),
                        pltpu.VMEM((tm, LANES), F32), pltpu.VMEM((LANES, tm), F32),
                        pltpu.VMEM((tm, D_MODEL), BF16), pltpu.VMEM((tm, LANES), F32),
                        pltpu.VMEM((tm, D_MODEL), F32), pltpu.VMEM((tm, D_MODEL), F32),
                        pltpu.SMEM((n_groups,), jnp.int32)],
        compiler_params=_cparams(("parallel", "arbitrary", "arbitrary"), 56),
        name="moe_grouped",
    )(x, ec["wr"], ec["br"], tril, tril.T, ec["wg"], ec["wu"], ec["wd"], ec["lng"], ec["lnb"])


def _lane_row(v, width=LANES):
    return jnp.pad(v.astype(F32), ((0, 0), (0, width - v.shape[1])))[:, None, :]


def kernel(x_prompt, x_sample, state_ssm_re, state_ssm_im, state_conv, state_delta, cache_k, cache_v, page_table,
           w_in, b_gate, ssm_a_re, ssm_a_im, ssm_log_dt, ssm_b_re, ssm_b_im, ssm_c_re, ssm_c_im, ssm_d, ssm_w_glu,
           ssm_b_glu, dn_conv_w, dn_a_log, dn_dt_bias, dn_norm_g, da_lambda, da_norm_g, w_branch, w_out, ln_g, ln_b,
           w_router, b_router, w_gate_e, w_up_e, w_down_e):
    bp, t, _ = x_prompt.shape
    ms = x_sample.shape[0]
    depth = w_in.shape[0]
    mp = bp * t

    o_dn, o_a, o_b, o_z, o_da, o_gate = 512, 2048, 2052, 2056, 2568, 4104
    w_main = _split_hbm(jnp.concatenate([w_in[:, :, o_dn:o_a], w_in[:, :, o_da:o_gate], w_in[:, :, :o_dn],
                                         w_in[:, :, o_z:o_da]], axis=2))
    w_ab = _split_hbm(jnp.pad(w_in[:, :, o_a:o_z], ((0, 0), (0, 0), (0, LANES - 8))))
    sc = _ssm_constants(ssm_a_re, ssm_a_im, ssm_log_dt, ssm_b_re, ssm_b_im, ssm_c_re, ssm_c_im, ssm_d)
    al = _lane_row(dn_a_log)
    dtb = _lane_row(dn_dt_bias)
    dc = dict(cw=jnp.pad(dn_conv_w.astype(F32).transpose(0, 2, 1), ((0, 0), (0, 8 - CONV_W), (0, 0))),
              al=al, dtb=dtb,
              alc=jnp.broadcast_to(jnp.pad(dn_a_log.astype(F32), ((0, 0), (0, 4)))[:, :, None], (depth, 8, DN_BLOCK)),
              dtbc=jnp.broadcast_to(jnp.pad(dn_dt_bias.astype(F32), ((0, 0), (0, 4)))[:, :, None],
                                    (depth, 8, DN_BLOCK)),
              ng=dn_norm_g.astype(F32)[:, None, :])
    lam_inits = [0.8 - 0.6 * math.exp(-0.3 * l) for l in range(depth)]
    lv = da_lambda.astype(F32)
    lam = (jnp.exp(jnp.sum(lv[:, 0] * lv[:, 1], axis=-1)) - jnp.exp(jnp.sum(lv[:, 2] * lv[:, 3], axis=-1))
           + jnp.asarray(lam_inits, F32))
    ac = dict(lam=jnp.broadcast_to(lam[:, None, None], (depth, 1, LANES)), ng=da_norm_g.astype(F32)[:, None, :])
    mc = dict(wg=_split_hbm(w_in[:, :, o_gate:]), bg=b_gate.astype(F32)[:, None, :],
              wglu=_split_hbm(ssm_w_glu), bglu=ssm_b_glu.astype(F32)[:, None, :],
              wbr=_split_hbm(w_branch), wout=_split_hbm(w_out), lng=ln_g.astype(F32), lnb=ln_b.astype(F32))
    ec = dict(wr=_split_hbm(jnp.pad(w_router, ((0, 0), (0, LANES - N_EXPERTS)))),
              br=jnp.pad(b_router.astype(F32), (0, LANES - N_EXPERTS))[None, :],
              wg=w_gate_e, wu=w_up_e, wd=w_down_e,
              lng=mc["lng"], lnb=mc["lnb"])

    n_pool = cache_k.shape[1]
    cache_k4 = cache_k.reshape(depth, n_pool, PAGE_SIZE * DA_HEADS, LANES)
    cache_v4 = cache_v.reshape(depth, n_pool, PAGE_SIZE * DA_HEADS, LANES)
    pt_flat = page_table.reshape(-1).astype(jnp.int32)

    xp = x_prompt.reshape(mp, D_MODEL)
    xs = x_sample.reshape(ms, D_MODEL)
    zero_h = jnp.zeros((bp * SSM_GROUPS, LANES), F32)
    zero_buf = jnp.zeros((bp, 8, DN_QKV), F32)
    zero_s = jnp.zeros((bp, DN_HEADS, DN_DK, DN_DK), F32)
    outs = {k: [] for k in ("p_re", "p_im", "p_conv", "p_delta",
                            "s_re", "s_im", "s_conv", "s_delta", "s_k", "s_v")}
    pkv = None
    for l in range(depth):
        u, ab = _project(xp, w_main, w_ab, l, min(1024, mp))
        ya, h_re, h_im = _ssm_prompt(u, sc, l, bp, t, zero_h)
        yb, s_fin = _dn_prompt(u, ab, dc, l, bp, t, zero_buf, zero_s)
        yc = _da_prompt(u, ac, l, bp, t, lam_inits[l])
        xm = _merge(xp, ya, yb, yc, mc, l, min(512, mp))
        xp = _moe_grouped(xm, ec, l, min(1024, mp))
        u3 = u.reshape(bp, t, N_MAIN)
        outs["p_re"].append(h_re)
        outs["p_im"].append(h_im)
        outs["p_conv"].append(u3[:, t - (CONV_W - 1):, :DN_QKV])
        outs["p_delta"].append(s_fin)
        pkv = _kv_out(u, l, depth, pkv)
        us, abs_ = _project(xs, w_main, w_ab, l, ms)
        ya_s, n_re, n_im = _ssm_step(us, sc, l, state_ssm_re[l].reshape(ms, -1).astype(F32),
                                     state_ssm_im[l].reshape(ms, -1).astype(F32))
        bufs = [state_conv[l, :, j, :].astype(F32) for j in range(CONV_W - 1)]
        yb_s, s_new = _dn_step(us, abs_, dc, l, bufs, state_delta)
        q3 = us[:, None, COL_DAQ * 512:(COL_DAQ + 1) * 512]
        k3 = us[:, None, COL_DAK * 512:(COL_DAK + 1) * 512]
        v3 = us[:, None, COL_DAV * 512:(COL_DAV + 1) * 512]
        yc_s = _da_step(q3, k3, v3, cache_k4, cache_v4, pt_flat, ac, l, lam_inits[l]).reshape(ms, 512)
        xm_s = _merge(xs, ya_s, yb_s, yc_s, mc, l, ms)
        xs = _moe(xm_s, ec, l, ms)
        outs["s_re"].append(n_re.reshape(ms, SSM_GROUPS, SSM_STATE))
        outs["s_im"].append(n_im.reshape(ms, SSM_GROUPS, SSM_STATE))
        outs["s_conv"].append(jnp.concatenate([state_conv[l, :, 1:, :].astype(F32), us[:, None, :DN_QKV]], axis=1))
        outs["s_delta"].append(s_new)
        outs["s_k"].append(k3.reshape(ms, 1, DA_HEADS, LANES))
        outs["s_v"].append(v3.reshape(ms, 1, DA_HEADS, LANES))

    st = lambda k, dt: jnp.stack(outs[k], axis=0).astype(dt)
    return (xp.reshape(bp, t, D_MODEL), xs.reshape(ms, 1, D_MODEL),
            st("p_re", state_ssm_re.dtype), st("p_im", state_ssm_im.dtype), st("p_conv", state_conv.dtype),
            st("p_delta", state_delta.dtype),
            pkv[0].reshape(depth, bp, t, DA_HEADS, LANES).astype(cache_k.dtype),
            pkv[1].reshape(depth, bp, t, DA_HEADS, LANES).astype(cache_v.dtype),
            st("s_re", state_ssm_re.dtype), st("s_im", state_ssm_im.dtype), st("s_conv", state_conv.dtype),
            st("s_delta", state_delta.dtype), st("s_k", cache_k.dtype), st("s_v", cache_v.dtype))
```
